```python
import math
import jax, jax.numpy as jnp
from jax import lax
import numpy as np

D_MODEL = 1024
BATCH = 2
SEQ = 8192
DEPTH = 4
DEC_BATCH = 128
DEC_SEQ = 1
PAST_LEN = 2048
PAGE_SIZE = 128

H_A = 4
DH_A = 64
DV_A = 2 * DH_A
H_B = 4
DK_B = 64
DV_B = 128
GATE_RANK = 16
GLA_TAU = 16.0
GLA_CHUNK = 64
H_C = 4
DH_C = 128
H_M = 4
DH_M = 128
N_MEM = 256
N_BRANCH = 4
BR_W = 512
D_FF = 2816
CONV_W = 3
Q_BLOCK = 128
EPS = 1e-6

COL_SIZES = (H_A * 2 * DH_A, H_A * 2 * DH_A, H_A * DV_A,
             H_B * DK_B, H_B * DK_B, H_B * DV_B, GATE_RANK, H_B * DV_B,
             H_C * DH_C, H_C * DH_C, H_C * DH_C,
             H_M * DH_M,
             N_BRANCH * D_MODEL)
IN_COLS = sum(COL_SIZES)
SPLIT_AT = tuple(sum(COL_SIZES[:i + 1]) for i in range(len(COL_SIZES) - 1))

kernel_name = 'hybrid_diff_gla_stickbreak_decoder_step'


def rms_norm(x, g):
    xf = x.astype(jnp.float32)
    y = xf * lax.rsqrt(jnp.mean(xf * xf, axis=-1, keepdims=True) + EPS)
    return (y * g.astype(jnp.float32)).astype(x.dtype)


def alibi_slopes():
    return jnp.asarray([2.0 ** (-8.0 * (h + 1) / H_A) for h in range(H_A)], dtype=jnp.float32)


def blocked_queries(fn, q, q_pos):
    b, n = q.shape[0], q.shape[1]
    nb = n // Q_BLOCK
    qb = jnp.swapaxes(q.reshape((b, nb, Q_BLOCK) + q.shape[2:]), 0, 1)
    pb = q_pos.reshape(nb, Q_BLOCK)
    out = lax.map(lambda a: fn(a[0], a[1]), (qb, pb))
    out = jnp.swapaxes(out, 0, 1)
    return out.reshape((b, n) + out.shape[3:])


def diff_attn_core(q, k, v, q_pos, k_pos, lam):
    s = jnp.einsum('bqhcd,bkhcd->bchqk', q, k).astype(jnp.float32) * (DH_A ** -0.5)
    dist = (q_pos[:, None] - k_pos[None, :]).astype(jnp.float32)
    s = s - alibi_slopes()[:, None, None] * dist
    s = jnp.where(dist >= 0, s, -jnp.inf)
    p = jax.nn.softmax(s, axis=-1)
    w = p[:, 0] - lam * p[:, 1]
    return jnp.einsum('bhqk,bkhd->bqhd', w.astype(v.dtype), v)


def stick_breaking_core(q, k, v, q_pos, k_pos):
    z = jnp.einsum('bqhd,bkhd->bhqk', q, k).astype(jnp.float32) * (DH_C ** -0.5)
    mask = k_pos[None, :] < q_pos[:, None]
    log_beta = jax.nn.log_sigmoid(z)
    log_1m = jnp.where(mask, jax.nn.log_sigmoid(-z), 0.0)
    after = lax.cumsum(log_1m, axis=z.ndim - 1, reverse=True) - log_1m
    a = jnp.where(mask, jnp.exp(log_beta + after), 0.0)
    return jnp.einsum('bhqk,bkhd->bqhd', a.astype(v.dtype), v)


def gla_chunked(q, k, v, log_a, s0):
    b, n = q.shape[0], q.shape[1]
    nc = n // GLA_CHUNK

    def to_chunks(t):
        return jnp.swapaxes(t.reshape((b, nc, GLA_CHUNK) + t.shape[2:]), 0, 1)

    causal = jnp.tril(jnp.ones((GLA_CHUNK, GLA_CHUNK), dtype=bool))

    def step(S, inp):
        qc, kc, vc, ac = inp
        cb = jnp.cumsum(ac.astype(jnp.float32), axis=1)
        diff = cb[:, :, None] - cb[:, None, :]
        decay = jnp.exp(jnp.where(causal[None, :, :, None, None], diff, -jnp.inf))
        attn = jnp.einsum('bthd,bshd,btshd->bhts', qc, kc, decay)
        o = (jnp.einsum('bhts,bshv->bthv', attn, vc)
             + jnp.einsum('bthd,bhdv->bthv', qc * jnp.exp(cb), S))
        c_last = cb[:, -1]
        S_new = (jnp.exp(c_last)[..., None] * S
                 + jnp.einsum('bshd,bshv->bhdv', kc * jnp.exp(c_last[:, None] - cb), vc))
        return S_new, o

    S, o = lax.scan(step, s0.astype(jnp.float32), (to_chunks(q), to_chunks(k), to_chunks(v), to_chunks(log_a)))
    o = jnp.swapaxes(o, 0, 1)
    return o.reshape((b, n) + o.shape[3:]), S


def gla_recurrent(q, k, v, log_a, s0):
    def step(S, inp):
        qt, kt, vt, at = inp
        S = jnp.exp(at.astype(jnp.float32))[..., None] * S + jnp.einsum('bhd,bhv->bhdv', kt, vt)
        return S, jnp.einsum('bhd,bhdv->bhv', qt, S)

    S, o = lax.scan(step, s0.astype(jnp.float32),
                    (jnp.swapaxes(q, 0, 1), jnp.swapaxes(k, 0, 1), jnp.swapaxes(v, 0, 1), jnp.swapaxes(log_a, 0, 1)))
    return jnp.swapaxes(o, 0, 1), S


def mem_attn(q, k, v):
    s = jnp.einsum('blhd,bmhd->bhlm', q, k).astype(jnp.float32) * (DH_M ** -0.5)
    p = jax.nn.softmax(s, axis=-1)
    return jnp.einsum('bhlm,bmhd->blhd', p.astype(v.dtype), v)


def memory_kv(mem, g, w_kv, kn_g):
    b, m = mem.shape[0], mem.shape[1]
    kv = jnp.einsum('bmd,dc->bmc', rms_norm(mem, g), w_kv)
    k, v = jnp.split(kv, 2, axis=-1)
    return rms_norm(k.reshape(b, m, H_M, DH_M), kn_g), v.reshape(b, m, H_M, DH_M)


def project(xn, w_in_l, b_gate_l, qn_a_g_l, kn_a_g_l, w_alpha2_l, b_alpha_l, qn_m_g_l):
    b, n = xn.shape[0], xn.shape[1]
    z = jnp.einsum('bld,dc->blc', xn, w_in_l)
    (dq, dk, dv, gq, gk, gv, ga, gg, sq, sk, sv, mq, gl) = jnp.split(z, SPLIT_AT, axis=-1)
    a_logit = (jnp.einsum('blr,rc->blc', ga, w_alpha2_l) + b_alpha_l).astype(jnp.float32)
    return {
        'dq': rms_norm(dq.reshape(b, n, H_A, 2, DH_A), qn_a_g_l),
        'dk': rms_norm(dk.reshape(b, n, H_A, 2, DH_A), kn_a_g_l),
        'dv': dv.reshape(b, n, H_A, DV_A),
        'gq': gq.reshape(b, n, H_B, DK_B) * (DK_B ** -0.5),
        'gk': gk.reshape(b, n, H_B, DK_B),
        'gv': gv.reshape(b, n, H_B, DV_B),
        'log_a': (jax.nn.log_sigmoid(a_logit) / GLA_TAU).reshape(b, n, H_B, DK_B),
        'gg': gg,
        'sq': sq.reshape(b, n, H_C, DH_C),
        'sk': sk.reshape(b, n, H_C, DH_C),
        'sv': sv.reshape(b, n, H_C, DH_C),
        'mq': rms_norm(mq.reshape(b, n, H_M, DH_M), qn_m_g_l),
        'gates': jax.nn.sigmoid(gl.reshape(b, n, N_BRANCH, D_MODEL) + b_gate_l),
    }


def diff_branch_out(o, subln_g, lam_init):
    b, n = o.shape[0], o.shape[1]
    return (rms_norm(o, subln_g) * (1.0 - lam_init)).reshape(b, n, H_A * DV_A)


def gla_branch_out(o, norm_g, gate):
    b, n = o.shape[0], o.shape[1]
    return rms_norm(o, norm_g).reshape(b, n, H_B * DV_B) * jax.nn.silu(gate)


def conv_ffn(h, conv_hist, w_up_l, conv_w_l, conv_b_l, w_down_l):
    n = h.shape[1]
    u, g = jnp.split(jnp.einsum('bld,df->blf', h, w_up_l), 2, axis=-1)
    g_ext = jnp.concatenate([conv_hist.astype(g.dtype), g], axis=1)
    gc = sum(g_ext[:, i:i + n] * conv_w_l[i] for i in range(CONV_W)) + conv_b_l
    y = jnp.einsum('blf,fd->bld', jax.nn.gelu(gc) * u, w_down_l)
    return y, g_ext[:, -(CONV_W - 1):]


def finish_layer(x, gates, branches, w_br_l, w_out_l, norm2_g_l, conv_hist, w_up_l, conv_w_l, conv_b_l, w_down_l):
    br = jnp.stack(branches, axis=2)
    proj = jnp.einsum('blnc,ncd->blnd', br, w_br_l)
    mixed = jnp.sum(gates * proj, axis=2)
    x = x + jnp.einsum('bld,de->ble', mixed, w_out_l).astype(x.dtype)
    f, hist = conv_ffn(rms_norm(x, norm2_g_l), conv_hist, w_up_l, conv_w_l, conv_b_l, w_down_l)
    return x + f.astype(x.dtype), hist


def gather_pages(pool, layer, page_table):
    g = pool[layer, page_table]
    return g.reshape((g.shape[0], g.shape[1] * g.shape[2]) + g.shape[3:])


def setup_inputs(seed: int = 0) -> dict:
    key = jax.random.key(seed)
    ks = list(jax.random.split(key, 48))

    def nrm(shape, scale=1.0):
        return jax.random.normal(ks.pop(), shape, jnp.float32) * scale

    def gain(shape):
        return 1.0 + nrm(shape, 0.01)

    n_pages = PAST_LEN // PAGE_SIZE
    n_used = DEC_BATCH * n_pages
    n_phys = n_used + (n_used + 3) // 4
    page_table = jax.random.permutation(ks.pop(), n_phys)[:n_used].reshape(DEC_BATCH, n_pages).astype(jnp.int32)
    out_scale = (2 * DEPTH) ** -0.5
    return {
        'x_prompt': nrm((BATCH, SEQ, D_MODEL)),
        'x_sample': nrm((DEC_BATCH, DEC_SEQ, D_MODEL)),
        'mem_prompt': nrm((BATCH, N_MEM, D_MODEL)),
        'cache_diff_k': nrm((DEPTH, n_phys, PAGE_SIZE, H_A, 2, DH_A)),
        'cache_diff_v': nrm((DEPTH, n_phys, PAGE_SIZE, H_A, DV_A)),
        'cache_sb_k': nrm((DEPTH, n_phys, PAGE_SIZE, H_C, DH_C)),
        'cache_sb_v': nrm((DEPTH, n_phys, PAGE_SIZE, H_C, DH_C)),
        'cache_mem_k': nrm((DEPTH, DEC_BATCH, N_MEM, H_M, DH_M)),
        'cache_mem_v': nrm((DEPTH, DEC_BATCH, N_MEM, H_M, DH_M)),
        'state_gla': nrm((DEPTH, DEC_BATCH, H_B, DK_B, DV_B)),
        'state_ffn_conv': nrm((DEPTH, DEC_BATCH, CONV_W - 1, D_FF)),
        'page_table': page_table,
        'norm1_g': gain((DEPTH, D_MODEL)),
        'w_in': nrm((DEPTH, D_MODEL, IN_COLS), D_MODEL ** -0.5),
        'b_gate': nrm((DEPTH, N_BRANCH, D_MODEL), 0.01),
        'qn_a_g': gain((DEPTH, DH_A)),
        'kn_a_g': gain((DEPTH, DH_A)),
        'lam_q1': nrm((DEPTH, DH_A), 0.1),
        'lam_k1': nrm((DEPTH, DH_A), 0.1),
        'lam_q2': nrm((DEPTH, DH_A), 0.1),
        'lam_k2': nrm((DEPTH, DH_A), 0.1),
        'subln_a_g': gain((DEPTH, DV_A)),
        'w_alpha2': nrm((DEPTH, GATE_RANK, H_B * DK_B), GATE_RANK ** -0.5),
        'b_alpha': nrm((DEPTH, H_B * DK_B), 0.01),
        'gla_norm_g': gain((DEPTH, DV_B)),
        'mem_norm_g': gain((DEPTH, D_MODEL)),
        'w_mem_kv': nrm((DEPTH, D_MODEL, 2 * H_M * DH_M), D_MODEL ** -0.5),
        'qn_m_g': gain((DEPTH, DH_M)),
        'kn_m_g': gain((DEPTH, DH_M)),
        'w_br': nrm((DEPTH, N_BRANCH, BR_W, D_MODEL), BR_W ** -0.5),
        'w_out': nrm((DEPTH, D_MODEL, D_MODEL), D_MODEL ** -0.5 * out_scale),
        'norm2_g': gain((DEPTH, D_MODEL)),
        'w_up': nrm((DEPTH, D_MODEL, 2 * D_FF), D_MODEL ** -0.5),
        'conv_w': nrm((DEPTH, CONV_W, D_FF), CONV_W ** -0.5),
        'conv_b': nrm((DEPTH, D_FF), 0.01),
        'w_down': nrm((DEPTH, D_FF, D_MODEL), D_FF ** -0.5 * out_scale),
    }


def reference(x_prompt, x_sample, mem_prompt, cache_diff_k, cache_diff_v, cache_sb_k, cache_sb_v,
              cache_mem_k, cache_mem_v, state_gla, state_ffn_conv, page_table,
              norm1_g, w_in, b_gate, qn_a_g, kn_a_g, lam_q1, lam_k1, lam_q2, lam_k2, subln_a_g,
              w_alpha2, b_alpha, gla_norm_g, mem_norm_g, w_mem_kv, qn_m_g, kn_m_g,
              w_br, w_out, norm2_g, w_up, conv_w, conv_b, w_down):
    xp, xs = x_prompt, x_sample
    bp, lp = xp.shape[0], xp.shape[1]
    bs, ls = xs.shape[0], xs.shape[1]
    past_len = page_table.shape[1] * cache_diff_k.shape[2]
    pos_p = jnp.arange(lp, dtype=jnp.int32)
    pos_s = past_len + jnp.arange(ls, dtype=jnp.int32)
    pos_all = jnp.arange(past_len + ls, dtype=jnp.int32)

    dk_p, dv_p, sk_p, sv_p, mk_p, mv_p, gla_p, conv_p = [], [], [], [], [], [], [], []
    dk_s, dv_s, sk_s, sv_s, gla_s, conv_s = [], [], [], [], [], []

    for l in range(DEPTH):
        lam_init = 0.8 - 0.6 * math.exp(-0.3 * l)
        lam = (jnp.exp(jnp.sum(lam_q1[l].astype(jnp.float32) * lam_k1[l].astype(jnp.float32)))
               - jnp.exp(jnp.sum(lam_q2[l].astype(jnp.float32) * lam_k2[l].astype(jnp.float32)))
               + lam_init)

        pr = project(rms_norm(xp, norm1_g[l]), w_in[l], b_gate[l], qn_a_g[l], kn_a_g[l],
                     w_alpha2[l], b_alpha[l], qn_m_g[l])
        mk, mv = memory_kv(mem_prompt, mem_norm_g[l], w_mem_kv[l], kn_m_g[l])
        o_a = blocked_queries(lambda qb, pb: diff_attn_core(qb, pr['dk'], pr['dv'], pb, pos_p, lam), pr['dq'], pos_p)
        o_b, s_new = gla_chunked(pr['gq'], pr['gk'], pr['gv'], pr['log_a'],
                                 jnp.zeros((bp, H_B, DK_B, DV_B), jnp.float32))
        o_c = blocked_queries(lambda qb, pb: stick_breaking_core(qb, pr['sk'], pr['sv'], pb, pos_p), pr['sq'], pos_p)
        o_m = mem_attn(pr['mq'], mk, mv)
        branches = [diff_branch_out(o_a, subln_a_g[l], lam_init),
                    gla_branch_out(o_b, gla_norm_g[l], pr['gg']),
                    o_c.reshape(bp, lp, H_C * DH_C),
                    o_m.reshape(bp, lp, H_M * DH_M)]
        xp, hist = finish_layer(xp, pr['gates'], branches, w_br[l], w_out[l], norm2_g[l],
                                jnp.zeros((bp, CONV_W - 1, D_FF), xp.dtype),
                                w_up[l], conv_w[l], conv_b[l], w_down[l])
        dk_p.append(pr['dk']); dv_p.append(pr['dv']); sk_p.append(pr['sk']); sv_p.append(pr['sv'])
        mk_p.append(mk); mv_p.append(mv); gla_p.append(s_new); conv_p.append(hist)

        ps = project(rms_norm(xs, norm1_g[l]), w_in[l], b_gate[l], qn_a_g[l], kn_a_g[l],
                     w_alpha2[l], b_alpha[l], qn_m_g[l])
        k_a = jnp.concatenate([gather_pages(cache_diff_k, l, page_table), ps['dk']], axis=1)
        v_a = jnp.concatenate([gather_pages(cache_diff_v, l, page_table), ps['dv']], axis=1)
        o_a = diff_attn_core(ps['dq'], k_a, v_a, pos_s, pos_all, lam)
        o_b, s_new = gla_recurrent(ps['gq'], ps['gk'], ps['gv'], ps['log_a'], state_gla[l])
        k_c = jnp.concatenate([gather_pages(cache_sb_k, l, page_table), ps['sk']], axis=1)
        v_c = jnp.concatenate([gather_pages(cache_sb_v, l, page_table), ps['sv']], axis=1)
        o_c = stick_breaking_core(ps['sq'], k_c, v_c, pos_s, pos_all)
        o_m = mem_attn(ps['mq'], cache_mem_k[l], cache_mem_v[l])
        branches = [diff_branch_out(o_a, subln_a_g[l], lam_init),
                    gla_branch_out(o_b, gla_norm_g[l], ps['gg']),
                    o_c.reshape(bs, ls, H_C * DH_C),
                    o_m.reshape(bs, ls, H_M * DH_M)]
        xs, hist = finish_layer(xs, ps['gates'], branches, w_br[l], w_out[l], norm2_g[l],
                                state_ffn_conv[l], w_up[l], conv_w[l], conv_b[l], w_down[l])
        dk_s.append(ps['dk']); dv_s.append(ps['dv']); sk_s.append(ps['sk']); sv_s.append(ps['sv'])
        gla_s.append(s_new); conv_s.append(hist)

    return (xp, xs,
            jnp.stack(dk_p), jnp.stack(dv_p), jnp.stack(sk_p), jnp.stack(sv_p),
            jnp.stack(mk_p), jnp.stack(mv_p), jnp.stack(gla_p), jnp.stack(conv_p),
            jnp.stack(dk_s), jnp.stack(dv_s), jnp.stack(sk_s), jnp.stack(sv_s),
            jnp.stack(gla_s), jnp.stack(conv_s))
```

```python
import functools
import math

import jax
import jax.numpy as jnp
from jax import lax
from jax.experimental import pallas as pl
from jax.experimental.pallas import tpu as pltpu

F32 = jnp.float32
BF16 = jnp.bfloat16
EPS = 1e-6

N_HEAD = 4
HEAD_W = 128
BR_W = N_HEAD * HEAD_W
DH_A = 64
DK_B = 64
GLA_W = N_HEAD * DK_B
GATE_RANK = 16
GLA_TAU = 16.0
N_BRANCH = 4
GLA_CHUNK = 128
GLA_SAFE_RANGE = 60.0
NEG_BIG = -1e30

VMEM_LIMIT_BYTES = 56 * 1024 * 1024


def _params(*sem):
    return pltpu.CompilerParams(dimension_semantics=sem, vmem_limit_bytes=VMEM_LIMIT_BYTES)


def _const_spec(shape):
    nd = len(shape)
    return pl.BlockSpec(shape, lambda *_: (0,) * nd, pipeline_mode=pl.Buffered(1))


def _full_spec(shape):
    nd = len(shape)
    return pl.BlockSpec(shape, lambda *_: (0,) * nd)


def _dot(a, b):
    return jnp.dot(a, b, preferred_element_type=F32)


def _dot_nt(a, b):
    return lax.dot_general(a, b, (((1,), (1,)), ((), ())), preferred_element_type=F32)


def _split_dot(x, m_bf16, parts):
    acc = None
    rem = x
    for _ in range(parts):
        piece = rem.astype(BF16)
        term = _dot(piece, m_bf16)
        acc = term if acc is None else acc + term
        rem = rem - piece.astype(F32)
    return acc


def _split_dot_left(m_bf16, x, parts):
    acc = None
    rem = x
    for _ in range(parts):
        piece = rem.astype(BF16)
        term = _dot(m_bf16, piece)
        acc = term if acc is None else acc + term
        rem = rem - piece.astype(F32)
    return acc


def _rms_rows(x, g):
    ms = jnp.mean(x * x, axis=-1, keepdims=True)
    return x * lax.rsqrt(ms + EPS) * g


def _log_sigmoid(x):
    return jnp.minimum(x, 0.0) - jnp.log1p(jnp.exp(-jnp.abs(x)))


def _sigmoid(x):
    return 1.0 / (1.0 + jnp.exp(-x))


def _head_rms(o, g):
    outs = []
    for h in range(N_HEAD):
        seg = o[:, h * HEAD_W:(h + 1) * HEAD_W]
        outs.append(_rms_rows(seg, g))
    return jnp.concatenate(outs, axis=-1)


_C_DQ, _C_DK, _C_DV = 0, 512, 1024
_C_GQ, _C_GK, _C_GV, _C_GG = 1536, 1792, 2048, 2560
_C_SQ, _C_SK, _C_SV = 3072, 3584, 4096
_C_MQ, _C_GA = 4608, 5120
_W_MAIN_COLS = 5248


def _proj_kernel(x_ref, g1_ref, w_ref, wa2_ref, ba_ref, qng_ref, kng_ref, mng_ref, g64_ref, g128_ref,
                 dq_b, dk_f, dk_b, dv_f, dv_b, gq_f, gk_f, gv_f, la_f, gg_f,
                 sq_b, sk_f, sk_b, sv_f, sv_b, mq_b):
    xb = _rms_rows(x_ref[...], g1_ref[...]).astype(BF16)

    def seg(off, width):
        return _dot(xb, w_ref[:, off:off + width])

    def group_norm(z, gmat_ref, gain_ref):
        ms = _split_dot(z * z, gmat_ref[...], 2)
        return z * lax.rsqrt(ms + EPS) * gain_ref[...]

    dq_b[...] = group_norm(seg(_C_DQ, BR_W), g64_ref, qng_ref).astype(BF16)
    dk = group_norm(seg(_C_DK, BR_W), g64_ref, kng_ref)
    dk_f[...] = dk
    dk_b[...] = dk.astype(BF16)
    dv = seg(_C_DV, BR_W)
    dv_f[...] = dv
    dv_b[...] = dv.astype(BF16)

    gq_f[...] = seg(_C_GQ, GLA_W) * (DK_B ** -0.5)
    gk_f[...] = seg(_C_GK, GLA_W)
    gv_f[...] = seg(_C_GV, BR_W)
    gg_f[...] = seg(_C_GG, BR_W)
    ga = seg(_C_GA, 128).astype(BF16)
    a_logit = _dot(ga, wa2_ref[...]) + ba_ref[...]
    la_f[...] = _log_sigmoid(a_logit) * (1.0 / GLA_TAU)

    sq_b[...] = seg(_C_SQ, BR_W).astype(BF16)
    sk = seg(_C_SK, BR_W)
    sk_f[...] = sk
    sk_b[...] = sk.astype(BF16)
    sv = seg(_C_SV, BR_W)
    sv_f[...] = sv
    sv_b[...] = sv.astype(BF16)

    mq_b[...] = group_norm(seg(_C_MQ, BR_W), g128_ref, mng_ref).astype(BF16)


def _proj(x2d, lw, consts, tm):
    n, d = x2d.shape
    row = lambda w: pl.BlockSpec((tm, w), lambda i: (i, 0))
    sds = lambda w, dt: jax.ShapeDtypeStruct((n, w), dt)
    outs = [(BR_W, BF16),
            (BR_W, F32), (BR_W, BF16), (BR_W, F32), (BR_W, BF16),
            (GLA_W, F32), (GLA_W, F32), (BR_W, F32), (GLA_W, F32), (BR_W, F32),
            (BR_W, BF16), (BR_W, F32), (BR_W, BF16), (BR_W, F32), (BR_W, BF16),
            (BR_W, BF16)]
    ins = [x2d, lw['g1'], lw['w_main'], lw['wa2'], lw['ba'], lw['qng'], lw['kng'], lw['mng'],
           consts['g64'], consts['g128']]
    return pl.pallas_call(
        _proj_kernel,
        out_shape=[sds(w, dt) for w, dt in outs],
        grid=(n // tm,),
        in_specs=[row(d)] + [_const_spec(a.shape) for a in ins[1:]],
        out_specs=[row(w) for w, _ in outs],
        compiler_params=_params("parallel"),
        name="proj",
    )(*ins)


def _lambda(lam_ref, lam_init):
    lp = lam_ref[...]
    s1 = jnp.sum(lp[0:1] * lp[1:2], axis=-1, keepdims=True)
    s2 = jnp.sum(lp[2:3] * lp[3:4], axis=-1, keepdims=True)
    return jnp.exp(s1) - jnp.exp(s2) + lam_init


def _diff_attn_kernel(slope_ref, q_ref, k_ref, v_ref, lam_ref, sg_ref, o_ref,
                      m1, l1, a1, m2, l2, a2, *, blk, lam_init):
    h = pl.program_id(1)
    i = pl.program_id(2)
    slope = slope_ref[h]
    q = q_ref[...]
    lane = lax.broadcasted_iota(jnp.int32, q.shape, 1)
    zero = jnp.zeros_like(q)
    q1 = jnp.where(lane < DH_A, q, zero)
    q2 = jnp.where(lane >= DH_A, q, zero)
    rel = (lax.broadcasted_iota(jnp.int32, (blk, blk), 0)
           - lax.broadcasted_iota(jnp.int32, (blk, blk), 1)).astype(F32)

    for m_ref, l_ref, a_ref in ((m1, l1, a1), (m2, l2, a2)):
        m_ref[...] = jnp.full(m_ref.shape, NEG_BIG, F32)
        l_ref[...] = jnp.zeros(l_ref.shape, F32)
        a_ref[...] = jnp.zeros(a_ref.shape, F32)

    def block(j, masked):
        ks = pl.multiple_of(j * blk, blk)
        k = k_ref[pl.ds(ks, blk), :]
        v = v_ref[pl.ds(ks, blk), :]
        dist = rel + ((i - j) * blk).astype(F32)
        bias = slope * dist
        for qc, m_ref, l_ref, a_ref in ((q1, m1, l1, a1), (q2, m2, l2, a2)):
            s = _dot_nt(qc, k) * (DH_A ** -0.5) - bias
            if masked:
                s = jnp.where(dist >= 0.0, s, NEG_BIG)
            m_old = m_ref[...]
            m_new = jnp.maximum(m_old, jnp.max(s, axis=-1, keepdims=True))
            alpha = jnp.exp(m_old - m_new)
            p = jnp.exp(s - m_new)
            l_ref[...] = alpha * l_ref[...] + jnp.sum(p, axis=-1, keepdims=True)
            a_ref[...] = alpha * a_ref[...] + _dot(p.astype(BF16), v)
            m_ref[...] = m_new

    def body(j, carry):
        block(j, False)
        return carry

    lax.fori_loop(0, i, body, 0)
    block(i, True)

    lam = _lambda(lam_ref, lam_init)
    o = a1[...] / l1[...] - lam * (a2[...] / l2[...])
    o_ref[...] = (_rms_rows(o, sg_ref[...]) * (1.0 - lam_init)).astype(BF16)


def _diff_attn(q, k, v, lw, consts, lam_init, blk):
    b, n, _ = q.shape
    qspec = pl.BlockSpec((None, blk, HEAD_W), lambda bi, hi, qi: (bi, qi, hi))
    kvspec = pl.BlockSpec((None, n, HEAD_W), lambda bi, hi, qi: (bi, 0, hi))
    vec = lambda w: pltpu.VMEM((blk, w), F32)
    return pl.pallas_call(
        functools.partial(_diff_attn_kernel, blk=blk, lam_init=lam_init),
        out_shape=jax.ShapeDtypeStruct((b, n, BR_W), BF16),
        grid=(b, N_HEAD, n // blk),
        in_specs=[pl.BlockSpec(memory_space=pltpu.SMEM), qspec, kvspec, kvspec,
                  _const_spec(lw['lam'].shape), _const_spec(lw['subln_g'].shape)],
        out_specs=qspec,
        scratch_shapes=[vec(1), vec(1), vec(HEAD_W), vec(1), vec(1), vec(HEAD_W)],
        compiler_params=_params("parallel", "parallel", "parallel"),
        name="diff_attn",
    )(consts['slopes'], q, k, v, lw['lam'], lw['subln_g'])


def _sb_attn_kernel(q_ref, k_ref, v_ref, u_ref, o_ref, r_ref, acc_ref, *, blk):
    i = pl.program_id(2)
    q = q_ref[...]
    row = lax.broadcasted_iota(jnp.int32, (blk, blk), 0)
    col = lax.broadcasted_iota(jnp.int32, (blk, blk), 1)
    r_ref[...] = jnp.zeros(r_ref.shape, F32)
    acc_ref[...] = jnp.zeros(acc_ref.shape, F32)

    def block(j, masked):
        ks = pl.multiple_of(j * blk, blk)
        k = k_ref[pl.ds(ks, blk), :]
        v = v_ref[pl.ds(ks, blk), :]
        z = _dot_nt(q, k) * (HEAD_W ** -0.5)
        sp = jnp.log1p(jnp.exp(-jnp.abs(z)))
        log_1m = -(jnp.maximum(z, 0.0) + sp)
        log_beta = jnp.minimum(z, 0.0) - sp
        if masked:
            valid = col < row
            log_1m = jnp.where(valid, log_1m, 0.0)
        after = _split_dot(log_1m, u_ref[...], 2) + r_ref[...]
        a = jnp.exp(log_beta + after)
        if masked:
            a = jnp.where(valid, a, 0.0)
        acc_ref[...] += _dot(a.astype(BF16), v)
        r_ref[...] += jnp.sum(log_1m, axis=-1, keepdims=True)

    block(i, True)

    def body(t, carry):
        block(i - 1 - t, False)
        return carry

    lax.fori_loop(0, i, body, 0)
    o_ref[...] = acc_ref[...].astype(BF16)


def _sb_attn(q, k, v, consts, blk):
    b, n, _ = q.shape
    qspec = pl.BlockSpec((None, blk, HEAD_W), lambda bi, hi, qi: (bi, qi, hi))
    kvspec = pl.BlockSpec((None, n, HEAD_W), lambda bi, hi, qi: (bi, 0, hi))
    u = consts['suffix'][blk]
    return pl.pallas_call(
        functools.partial(_sb_attn_kernel, blk=blk),
        out_shape=jax.ShapeDtypeStruct((b, n, BR_W), BF16),
        grid=(b, N_HEAD, n // blk),
        in_specs=[qspec, kvspec, kvspec, _const_spec(u.shape)],
        out_specs=qspec,
        scratch_shapes=[pltpu.VMEM((blk, 1), F32), pltpu.VMEM((blk, HEAD_W), F32)],
        compiler_params=_params("parallel", "parallel", "parallel"),
        name="sb_attn",
    )(q, k, v, u)


def _lane_head(shape, width):
    return lax.broadcasted_iota(jnp.int32, shape, 1) // width


def _column(xt, t):
    lane = lax.broadcasted_iota(jnp.int32, xt.shape, 1)
    return jnp.sum(jnp.where(lane == t, xt, 0.0), axis=1, keepdims=True)


def _gla_token(state, decay_col, k_col, q_col, v_row):
    v_rows = jnp.concatenate(
        [jnp.broadcast_to(v_row[:, h * HEAD_W:(h + 1) * HEAD_W], (DK_B, HEAD_W)) for h in range(N_HEAD)], axis=0)
    state = decay_col * state + k_col * v_rows
    qs = q_col * state
    o = jnp.concatenate(
        [jnp.sum(qs[h * DK_B:(h + 1) * DK_B], axis=0, keepdims=True) for h in range(N_HEAD)], axis=-1)
    return state, o


def _gla_finish(o, gg, ng):
    return (_head_rms(o, ng) * (gg * _sigmoid(gg))).astype(BF16)


def _gla_chunk_kernel(q_ref, k_ref, v_ref, la_ref, gg_ref, ng_ref, ltri_ref, o_ref, s_out_ref, s_scr, o_scr):
    c = GLA_CHUNK

    @pl.when(pl.program_id(1) == 0)
    def _():
        s_scr[...] = jnp.zeros(s_scr.shape, F32)

    q = q_ref[...]
    k = k_ref[...]
    la = la_ref[...]
    cb = _split_dot_left(ltri_ref[...], la, 3)
    cbm = cb - cb[c // 2 - 1:c // 2, :]
    spread = jnp.max(jnp.abs(cbm))

    def factored():
        vb = v_ref[...].astype(BF16)
        state = s_scr[...]
        state_b = state.astype(BF16)
        qt = q * jnp.exp(cbm)
        kt = (k * jnp.exp(-cbm)).astype(BF16)
        qs = q * jnp.exp(cb)
        cb_t = cb.T
        last = cb_t[:, c - 1:c]
        kh_t = (k.T * jnp.exp(last - cb_t)).astype(BF16)
        head = _lane_head((c, GLA_W), DK_B)
        causal = (lax.broadcasted_iota(jnp.int32, (c, c), 1) <= lax.broadcasted_iota(jnp.int32, (c, c), 0))
        outs, upd = [], []
        for h in range(N_HEAD):
            sel = head == h
            vh = vb[:, h * HEAD_W:(h + 1) * HEAD_W]
            attn = jnp.where(causal, _dot_nt(jnp.where(sel, qt, 0.0).astype(BF16), kt), 0.0)
            outs.append(_dot(attn.astype(BF16), vh) + _dot(jnp.where(sel, qs, 0.0).astype(BF16), state_b))
            upd.append(_dot(kh_t[h * DK_B:(h + 1) * DK_B, :], vh))
        o_scr[...] = jnp.concatenate(outs, axis=-1)
        s_scr[...] = jnp.exp(last) * state + jnp.concatenate(upd, axis=0)

    def recurrent():
        q_t = q.T
        k_t = k.T
        decay_t = jnp.exp(la).T

        def step(t, state):
            state, o = _gla_token(state, _column(decay_t, t), _column(k_t, t), _column(q_t, t),
                                  v_ref[pl.ds(t, 1), :])
            o_scr[pl.ds(t, 1), :] = o
            return state

        s_scr[...] = lax.fori_loop(0, c, step, s_scr[...])

    lax.cond(spread <= GLA_SAFE_RANGE, factored, recurrent)

    o_ref[...] = _gla_finish(o_scr[...], gg_ref[...], ng_ref[...])
    s_out_ref[...] = s_scr[...]


def _gla_chunked(gq, gk, gv, la, gg, lw, consts):
    b, n, _ = gq.shape
    c = GLA_CHUNK
    blk = lambda w: pl.BlockSpec((None, c, w), lambda bi, ci: (bi, ci, 0))
    ltri = consts['ltri']
    return pl.pallas_call(
        _gla_chunk_kernel,
        out_shape=[jax.ShapeDtypeStruct((b, n, BR_W), BF16),
                   jax.ShapeDtypeStruct((b, GLA_W, HEAD_W), F32)],
        grid=(b, n // c),
        in_specs=[blk(GLA_W), blk(GLA_W), blk(BR_W), blk(GLA_W), blk(BR_W),
                  _const_spec(lw['gla_ng'].shape), _const_spec(ltri.shape)],
        out_specs=[blk(BR_W), pl.BlockSpec((None, GLA_W, HEAD_W), lambda bi, ci: (bi, 0, 0))],
        scratch_shapes=[pltpu.VMEM((GLA_W, HEAD_W), F32), pltpu.VMEM((c, BR_W), F32)],
        compiler_params=_params("parallel", "arbitrary"),
        name="gla_chunked",
    )(gq, gk, gv, la, gg, lw['gla_ng'], ltri)


def _gla_step_kernel(q_ref, k_ref, la_ref, v_ref, gg_ref, ng_ref, s_ref, o_ref, s_out_ref, qt_scr, kt_scr, dt_scr):
    b = pl.program_id(0)

    @pl.when(b == 0)
    def _():
        qt_scr[...] = q_ref[...].T
        kt_scr[...] = k_ref[...].T
        dt_scr[...] = jnp.exp(la_ref[...]).T

    state, o = _gla_token(s_ref[...], _column(dt_scr[...], b), _column(kt_scr[...], b), _column(qt_scr[...], b),
                          v_ref[pl.ds(b, 1), :])
    s_out_ref[...] = state
    o_ref[...] = _gla_finish(o, gg_ref[pl.ds(b, 1), :], ng_ref[...])


def _gla_step(gq, gk, la, gv, gg, state, lw):
    nb = gq.shape[0]
    sspec = pl.BlockSpec((None, GLA_W, HEAD_W), lambda bi: (bi, 0, 0))
    t_scr = pltpu.VMEM((GLA_W, nb), F32)
    return pl.pallas_call(
        _gla_step_kernel,
        out_shape=[jax.ShapeDtypeStruct((nb, 1, BR_W), BF16),
                   jax.ShapeDtypeStruct((nb, GLA_W, HEAD_W), F32)],
        grid=(nb,),
        in_specs=[_const_spec(gq.shape), _const_spec(gk.shape), _const_spec(la.shape), _const_spec(gv.shape),
                  _const_spec(gg.shape), _const_spec(lw['gla_ng'].shape), sspec],
        out_specs=[pl.BlockSpec((None, 1, BR_W), lambda bi: (bi, 0, 0)), sspec],
        scratch_shapes=[t_scr, t_scr, t_scr],
        compiler_params=_params("arbitrary"),
        name="gla_step",
    )(gq, gk, la, gv, gg, lw['gla_ng'], state)


def _mem_kv_kernel(m_ref, g_ref, w_ref, kng_ref, g128_ref, k_f, v_f):
    xb = _rms_rows(m_ref[...], g_ref[...]).astype(BF16)
    k = _dot(xb, w_ref[:, 0:BR_W])
    ms = _split_dot(k * k, g128_ref[...], 2)
    k_f[...] = k * lax.rsqrt(ms + EPS) * kng_ref[...]
    v_f[...] = _dot(xb, w_ref[:, BR_W:2 * BR_W])


def _mem_kv(mem2d, lw, consts):
    n = mem2d.shape[0]
    ins = [mem2d, lw['mem_g'], lw['w_mem'], lw['kmg'], consts['g128']]
    return pl.pallas_call(
        _mem_kv_kernel,
        out_shape=[jax.ShapeDtypeStruct((n, BR_W), F32)] * 2,
        grid=(1,),
        in_specs=[_const_spec(a.shape) for a in ins],
        out_specs=[_full_spec((n, BR_W))] * 2,
        compiler_params=_params("arbitrary"),
        name="mem_kv",
    )(*ins)


def _mem_attn_kernel(q_ref, k_ref, v_ref, o_ref, *, rows):
    q = q_ref[...]
    if rows < 8:
        q = jnp.broadcast_to(q[0:1], (8, q.shape[1]))
    kb = k_ref[...].astype(BF16)
    vb = v_ref[...].astype(BF16)
    outs = []
    for h in range(N_HEAD):
        sl = slice(h * HEAD_W, (h + 1) * HEAD_W)
        s = _dot_nt(q[:, sl], kb[:, sl]) * (HEAD_W ** -0.5)
        p = jnp.exp(s - jnp.max(s, axis=-1, keepdims=True))
        o = _dot(p.astype(BF16), vb[:, sl])
        outs.append(o / jnp.sum(p, axis=-1, keepdims=True))
    o_ref[...] = jnp.concatenate(outs, axis=-1)[0:rows].astype(BF16)


def _mem_attn(q, k, v, layer, tm):
    b, n, _ = q.shape
    n_mem = k.shape[2]
    qspec = pl.BlockSpec((None, tm, BR_W), lambda bi, ri: (bi, ri, 0))
    kvspec = pl.BlockSpec((None, None, n_mem, BR_W), lambda bi, ri: (layer, bi, 0, 0))
    return pl.pallas_call(
        functools.partial(_mem_attn_kernel, rows=tm),
        out_shape=jax.ShapeDtypeStruct((b, n, BR_W), BF16),
        grid=(b, n // tm),
        in_specs=[qspec, kvspec, kvspec],
        out_specs=qspec,
        compiler_params=_params("parallel", "parallel"),
        name="mem_attn",
    )(q, k, v)


def _merge_kernel(x_ref, g1_ref, wg_ref, bg_ref, b0_ref, b1_ref, b2_ref, b3_ref, wbr_ref, wout_ref, o_ref):
    x = x_ref[...]
    d = x.shape[1]
    xb = _rms_rows(x, g1_ref[...]).astype(BF16)
    mixed = None
    for i, br_ref in enumerate((b0_ref, b1_ref, b2_ref, b3_ref)):
        gate = _sigmoid(_dot(xb, wg_ref[:, i * d:(i + 1) * d]) + bg_ref[:, i * d:(i + 1) * d])
        term = gate * _dot(br_ref[...], wbr_ref[i])
        mixed = term if mixed is None else mixed + term
    o_ref[...] = x + _dot(mixed.astype(BF16), wout_ref[...])


def _merge(x2d, branches, lw, tm):
    n, d = x2d.shape
    row = lambda w: pl.BlockSpec((tm, w), lambda i: (i, 0))
    ws = [lw['g1'], lw['w_gate'], lw['b_gate']]
    ws2 = [lw['w_br'], lw['w_out']]
    return pl.pallas_call(
        _merge_kernel,
        out_shape=jax.ShapeDtypeStruct((n, d), F32),
        grid=(n // tm,),
        in_specs=[row(d)] + [_const_spec(a.shape) for a in ws] + [row(BR_W)] * N_BRANCH
                 + [_const_spec(a.shape) for a in ws2],
        out_specs=row(d),
        compiler_params=_params("parallel"),
        name="merge",
    )(x2d, *ws, *branches, *ws2)


FF_CHUNK = 256
CARRY_ROWS = 8


def _gelu_tanh(x):
    return 0.5 * x * (1.0 + jnp.tanh(math.sqrt(2.0 / math.pi) * (x + 0.044715 * (x * x * x))))


def _ffn_kernel(*refs, seq_mode, d_ff):
    if seq_mode:
        x_ref, g2_ref, wu_ref, wg_ref, cw_ref, cb_ref, wd_ref, o_ref, hist_ref, carry = refs
    else:
        x_ref, g2_ref, wu_ref, wg_ref, cw_ref, cb_ref, wd_ref, h0_ref, h1_ref, o_ref, g_out_ref = refs
    x = x_ref[...]
    tm = x.shape[0]
    hb = _rms_rows(x, g2_ref[...]).astype(BF16)

    if seq_mode:
        @pl.when(pl.program_id(1) == 0)
        def _():
            carry[...] = jnp.zeros(carry.shape, F32)
        rowid = lax.broadcasted_iota(jnp.int32, (tm, FF_CHUNK), 0)

    y = None
    for c0 in range(0, d_ff, FF_CHUNK):
        cs = slice(c0, c0 + FF_CHUNK)
        u = _dot(hb, wu_ref[:, cs])
        g = _dot(hb, wg_ref[:, cs])
        if seq_mode:
            prev = carry[:, cs]
            p1 = prev[CARRY_ROWS - 1:CARRY_ROWS]
            p2 = prev[CARRY_ROWS - 2:CARRY_ROWS - 1]
            g1 = jnp.where(rowid == 0, p1, pltpu.roll(g, 1, 0))
            g2 = jnp.where(rowid == 0, p2, jnp.where(rowid == 1, p1, pltpu.roll(g, 2, 0)))
            carry[:, cs] = g[tm - CARRY_ROWS:tm]
        else:
            g1 = h1_ref[:, cs]
            g2 = h0_ref[:, cs]
            g_out_ref[:, cs] = g
        gc = g2 * cw_ref[0:1, cs] + g1 * cw_ref[1:2, cs] + g * cw_ref[2:3, cs] + cb_ref[:, cs]
        term = _dot((_gelu_tanh(gc) * u).astype(BF16), wd_ref[cs, :])
        y = term if y is None else y + term
    o_ref[...] = x + y
    if seq_mode:
        hist_ref[...] = carry[CARRY_ROWS - 2:CARRY_ROWS, :]


def _ffn_seq(x3d, lw, tm):
    b, n, d = x3d.shape
    d_ff = lw['w_u'].shape[1]
    ws = [lw['g2'], lw['w_u'], lw['w_g'], lw['conv_w'], lw['conv_b'], lw['w_down']]
    xspec = pl.BlockSpec((None, tm, d), lambda bi, ri: (bi, ri, 0))
    return pl.pallas_call(
        functools.partial(_ffn_kernel, seq_mode=True, d_ff=d_ff),
        out_shape=[jax.ShapeDtypeStruct((b, n, d), F32), jax.ShapeDtypeStruct((b, 2, d_ff), F32)],
        grid=(b, n // tm),
        in_specs=[xspec] + [_const_spec(a.shape) for a in ws],
        out_specs=[xspec, pl.BlockSpec((None, 2, d_ff), lambda bi, ri: (bi, 0, 0))],
        scratch_shapes=[pltpu.VMEM((CARRY_ROWS, d_ff), F32)],
        compiler_params=_params("parallel", "arbitrary"),
        name="ffn_seq",
    )(x3d, *ws)


def _ffn_step(x2d, hist0, hist1, lw):
    n, d = x2d.shape
    d_ff = lw['w_u'].shape[1]
    ws = [lw['g2'], lw['w_u'], lw['w_g'], lw['conv_w'], lw['conv_b'], lw['w_down']]
    ins = [x2d, *ws, hist0, hist1]
    return pl.pallas_call(
        functools.partial(_ffn_kernel, seq_mode=False, d_ff=d_ff),
        out_shape=[jax.ShapeDtypeStruct((n, d), F32), jax.ShapeDtypeStruct((n, d_ff), F32)],
        grid=(1,),
        in_specs=[_const_spec(a.shape) for a in ins],
        out_specs=[_full_spec((n, d)), _full_spec((n, d_ff))],
        compiler_params=_params("arbitrary"),
        name="ffn_step",
    )(*ins)


DEC_ROWS = 8


def _decode_attn_kernel(pt_ref, *refs, pages_per_step, n_pages, page, lam_init):
    p = pages_per_step
    (slope_ref, dq_ref, dk_ref, dv_ref, sq_ref, lam_ref, sg_ref, u_ref) = refs[:8]
    kd = refs[8:8 + p]
    vd = refs[8 + p:8 + 2 * p]
    ks = refs[8 + 2 * p:8 + 3 * p]
    vs = refs[8 + 3 * p:8 + 4 * p]
    od_ref, os_ref, m_ref, l_ref, acc_ref, r_ref, accs_ref = refs[8 + 4 * p:]
    j = pl.program_id(1)
    shape = (DEC_ROWS, BR_W)
    row = lax.broadcasted_iota(jnp.int32, shape, 0)
    lane = lax.broadcasted_iota(jnp.int32, shape, 1)
    dq = jnp.broadcast_to(dq_ref[...].astype(F32), shape)
    qd_f = jnp.where(lane // DH_A == row, dq, 0.0)
    qd = qd_f.astype(BF16)
    sq = jnp.broadcast_to(sq_ref[...].astype(F32), shape)
    qs = jnp.where(lane // HEAD_W == row, sq, 0.0).astype(BF16)
    slopes = slope_ref[...][:, 0:1]
    past = n_pages * page

    @pl.when(j == 0)
    def _():
        s_self = jnp.sum(qd_f * dk_ref[...], axis=-1, keepdims=True) * (DH_A ** -0.5)
        m_ref[...] = s_self
        l_ref[...] = jnp.ones(l_ref.shape, F32)
        acc_ref[...] = jnp.broadcast_to(dv_ref[...], shape)
        r_ref[...] = jnp.zeros(r_ref.shape, F32)
        accs_ref[...] = jnp.zeros(accs_ref.shape, F32)

    col = lax.broadcasted_iota(jnp.int32, (DEC_ROWS, page), 1)
    for t in range(p):
        first = (j * p + t) * page
        dist = (past - first - col).astype(F32)
        s = _dot_nt(qd, kd[t][...].astype(BF16)) * (DH_A ** -0.5) - slopes * dist
        m_old = m_ref[...]
        m_new = jnp.maximum(m_old, jnp.max(s, axis=-1, keepdims=True))
        alpha = jnp.exp(m_old - m_new)
        pr = jnp.exp(s - m_new)
        l_ref[...] = alpha * l_ref[...] + jnp.sum(pr, axis=-1, keepdims=True)
        acc_ref[...] = alpha * acc_ref[...] + _dot(pr.astype(BF16), vd[t][...].astype(BF16))
        m_ref[...] = m_new
        z = _dot_nt(qs, ks[t][...].astype(BF16)) * (HEAD_W ** -0.5)
        sp = jnp.log1p(jnp.exp(-jnp.abs(z)))
        log_1m = -(jnp.maximum(z, 0.0) + sp)
        log_beta = jnp.minimum(z, 0.0) - sp
        after = _split_dot(log_1m, u_ref[...], 2) + r_ref[...]
        a = jnp.exp(log_beta + after)
        accs_ref[...] += _dot(a.astype(BF16), vs[t][...].astype(BF16))
        r_ref[...] += jnp.sum(log_1m, axis=-1, keepdims=True)

    @pl.when(j == pl.num_programs(1) - 1)
    def _():
        lam = _lambda(lam_ref, lam_init)
        coef = jnp.where(row % 2 == 0, 1.0, -lam)
        coef = jnp.where(lane // HEAD_W == row // 2, coef, 0.0)
        o = jnp.sum(acc_ref[...] / l_ref[...] * coef, axis=0, keepdims=True)
        od_ref[...] = (_head_rms(o, sg_ref[...]) * (1.0 - lam_init)).astype(BF16)
        keep = jnp.where(lane // HEAD_W == row, accs_ref[...], 0.0)
        os_ref[...] = jnp.sum(keep, axis=0, keepdims=True).astype(BF16)


def _decode_attn(page_table, dq, dk, dv, sq, caches, layer, lw, consts, lam_init, pages_per_step):
    nb, n_pages = page_table.shape
    cdk, cdv, csk, csv = caches
    page = cdk.shape[2]
    p = pages_per_step
    steps = n_pages // p
    rowspec = pl.BlockSpec((None, 1, BR_W), lambda b, j, pt: (b, 0, 0))
    cspec = lambda a: pl.BlockSpec(a.shape, lambda b, j, pt: (0,) * a.ndim)

    def page_spec(t, reverse):
        def index(b, j, pt):
            logical = j * p + t
            if reverse:
                logical = n_pages - 1 - logical
            return (layer, pt[b, logical], 0, 0)
        return pl.BlockSpec((None, None, page, BR_W), index)

    u = consts['suffix'][page]
    small = [consts['slope_rows'], dq, dk, dv, sq, lw['lam'], lw['subln_g'], u]
    in_specs = ([cspec(small[0])] + [rowspec] * 4 + [cspec(a) for a in small[5:]]
                + [page_spec(t, False) for t in range(p)] * 2 + [page_spec(t, True) for t in range(p)] * 2)
    vec = lambda w: pltpu.VMEM((DEC_ROWS, w), F32)
    grid_spec = pltpu.PrefetchScalarGridSpec(
        num_scalar_prefetch=1, grid=(nb, steps), in_specs=in_specs,
        out_specs=[rowspec, rowspec],
        scratch_shapes=[vec(1), vec(1), vec(BR_W), vec(1), vec(BR_W)])
    return pl.pallas_call(
        functools.partial(_decode_attn_kernel, pages_per_step=p, n_pages=n_pages, page=page, lam_init=lam_init),
        out_shape=[jax.ShapeDtypeStruct((nb, 1, BR_W), BF16)] * 2,
        grid_spec=grid_spec,
        compiler_params=_params("parallel", "arbitrary"),
        name="decode_attn",
    )(page_table, *small, *([cdk] * p), *([cdv] * p), *([csk] * p), *([csv] * p))


def _block_diag_mean(width, group):
    idx = jnp.arange(width) // group
    return jnp.where(idx[:, None] == idx[None, :], 1.0 / group, 0.0).astype(BF16)


def _suffix_matrix(n):
    idx = jnp.arange(n)
    return (idx[:, None] > idx[None, :]).astype(BF16)


def _row_tile(n, cap):
    t = min(n, cap)
    while n % t:
        t //= 2
    return t


def kernel(x_prompt, x_sample, mem_prompt, cache_diff_k, cache_diff_v, cache_sb_k, cache_sb_v, cache_mem_k, cache_mem_v, state_gla, state_ffn_conv, page_table, norm1_g, w_in, b_gate, qn_a_g, kn_a_g, lam_q1, lam_k1, lam_q2, lam_k2, subln_a_g, w_alpha2, b_alpha, gla_norm_g, mem_norm_g, w_mem_kv, qn_m_g, kn_m_g, w_br, w_out, norm2_g, w_up, conv_w, conv_b, w_down):
    depth, d_model, _ = w_in.shape
    bp, lp, _ = x_prompt.shape
    bs, ls, _ = x_sample.shape
    assert ls == 1, "the sample group decodes one token per sequence"
    n_mem = mem_prompt.shape[1]
    n_phys, page = cache_diff_k.shape[1], cache_diff_k.shape[2]
    n_pages = page_table.shape[1]
    d_ff = w_down.shape[1]

    attn_blk = _row_tile(lp, 256)
    pages_per_step = _row_tile(n_pages, 4)
    consts = {
        'g64': _block_diag_mean(BR_W, DH_A),
        'g128': _block_diag_mean(BR_W, HEAD_W),
        'ltri': (jnp.arange(GLA_CHUNK)[:, None] >= jnp.arange(GLA_CHUNK)[None, :]).astype(BF16),
        'suffix': {n: _suffix_matrix(n) for n in {attn_blk, page}},
        'slopes': jnp.asarray([2.0 ** (-8.0 * (h + 1) / N_HEAD) for h in range(N_HEAD)], F32),
    }
    consts['slope_rows'] = jnp.broadcast_to(jnp.repeat(consts['slopes'], 2)[:, None], (DEC_ROWS, 128))

    caches = [c.reshape(depth, n_phys, page, BR_W) for c in (cache_diff_k, cache_diff_v, cache_sb_k, cache_sb_v)]
    mem_k = cache_mem_k.reshape(depth, bs, n_mem, BR_W)
    mem_v = cache_mem_v.reshape(depth, bs, n_mem, BR_W)

    splits = [0]
    for w in (512, 512, 512, 256, 256, 512, GATE_RANK, 512, 512, 512, 512, 512, N_BRANCH * d_model):
        splits.append(splits[-1] + w)
    (c_dq, c_dk, c_dv, c_gq, c_gk, c_gv, c_ga, c_gg, c_sq, c_sk, c_sv, c_mq, c_gl, c_end) = splits

    def tile_gain(g, reps):
        return jnp.tile(g, reps).reshape(1, -1)

    xp = x_prompt.reshape(bp * lp, d_model)
    xs = x_sample.reshape(bs * ls, d_model)
    mem2d = mem_prompt.reshape(bp * n_mem, d_model)
    tm_p = _row_tile(bp * lp, 256)
    tm_s = _row_tile(bs, 256)
    tm_f = _row_tile(lp, 256)

    outs = [[] for _ in range(14)]
    for l in range(depth):
        lam_init = 0.8 - 0.6 * math.exp(-0.3 * l)
        wl = w_in[l]
        ga_pad = jnp.zeros((d_model, 128 - GATE_RANK), wl.dtype)
        lw = {
            'g1': norm1_g[l].reshape(1, -1),
            'w_main': jnp.concatenate(
                [wl[:, c_dq:c_ga], wl[:, c_gg:c_gl], wl[:, c_ga:c_gg], ga_pad], axis=1).astype(BF16),
            'wa2': jnp.concatenate([w_alpha2[l], jnp.zeros((128 - GATE_RANK, GLA_W), F32)], axis=0).astype(BF16),
            'ba': b_alpha[l].reshape(1, -1),
            'qng': tile_gain(qn_a_g[l], 2 * N_HEAD),
            'kng': tile_gain(kn_a_g[l], 2 * N_HEAD),
            'mng': tile_gain(qn_m_g[l], N_HEAD),
            'kmg': tile_gain(kn_m_g[l], N_HEAD),
            'lam': jnp.stack([lam_q1[l], lam_k1[l], lam_q2[l], lam_k2[l]]),
            'subln_g': subln_a_g[l].reshape(1, -1),
            'gla_ng': gla_norm_g[l].reshape(1, -1),
            'mem_g': mem_norm_g[l].reshape(1, -1),
            'w_mem': w_mem_kv[l].astype(BF16),
            'w_gate': wl[:, c_gl:c_end].astype(BF16),
            'b_gate': b_gate[l].reshape(1, -1),
            'w_br': w_br[l].astype(BF16),
            'w_out': w_out[l].astype(BF16),
            'g2': norm2_g[l].reshape(1, -1),
            'w_u': w_up[l][:, :d_ff].astype(BF16),
            'w_g': w_up[l][:, d_ff:].astype(BF16),
            'conv_w': conv_w[l],
            'conv_b': conv_b[l].reshape(1, -1),
            'w_down': w_down[l].astype(BF16),
        }
        assert lw['w_main'].shape[1] == _W_MAIN_COLS

        (dq_b, dk_f, dk_b, dv_f, dv_b, gq, gk, gv, la, gg,
         sq_b, sk_f, sk_b, sv_f, sv_b, mq_b) = _proj(xp, lw, consts, tm_p)
        r3 = lambda a: a.reshape(bp, lp, a.shape[-1])
        mk_f, mv_f = _mem_kv(mem2d, lw, consts)
        o_a = _diff_attn(r3(dq_b), r3(dk_b), r3(dv_b), lw, consts, lam_init, attn_blk)
        o_b, s_new = _gla_chunked(r3(gq), r3(gk), r3(gv), r3(la), r3(gg), lw, consts)
        o_c = _sb_attn(r3(sq_b), r3(sk_b), r3(sv_b), consts, attn_blk)
        o_m = _mem_attn(r3(mq_b), mk_f.reshape(1, bp, n_mem, BR_W), mv_f.reshape(1, bp, n_mem, BR_W), 0, tm_f)
        flat = lambda a: a.reshape(bp * lp, BR_W)
        x1 = _merge(xp, [flat(o_a), flat(o_b), flat(o_c), flat(o_m)], lw, tm_p)
        x2, hist = _ffn_seq(x1.reshape(bp, lp, d_model), lw, tm_f)
        xp = x2.reshape(bp * lp, d_model)
        prompt_new = [dk_f.reshape(bp, lp, N_HEAD, 2, DH_A), dv_f.reshape(bp, lp, N_HEAD, HEAD_W),
                      sk_f.reshape(bp, lp, N_HEAD, HEAD_W), sv_f.reshape(bp, lp, N_HEAD, HEAD_W),
                      mk_f.reshape(bp, n_mem, N_HEAD, HEAD_W), mv_f.reshape(bp, n_mem, N_HEAD, HEAD_W),
                      s_new.reshape(bp, N_HEAD, DK_B, HEAD_W), hist]

        (dq_b, dk_f, dk_b, dv_f, dv_b, gq, gk, gv, la, gg,
         sq_b, sk_f, sk_b, sv_f, sv_b, mq_b) = _proj(xs, lw, consts, tm_s)
        r1 = lambda a: a.reshape(bs, 1, a.shape[-1])
        o_a, o_c = _decode_attn(page_table, r1(dq_b), r1(dk_f), r1(dv_f), r1(sq_b), caches, l, lw, consts,
                                lam_init, pages_per_step)
        o_b, s_new = _gla_step(gq, gk, la, gv, gg, state_gla[l].reshape(bs, GLA_W, HEAD_W), lw)
        o_m = _mem_attn(r1(mq_b), mem_k, mem_v, l, 1)
        flat = lambda a: a.reshape(bs, BR_W)
        x1 = _merge(xs, [flat(o_a), flat(o_b), flat(o_c), flat(o_m)], lw, tm_s)
        hist_in = state_ffn_conv[l]
        xs, g_new = _ffn_step(x1, hist_in[:, 0], hist_in[:, 1], lw)
        sample_new = [dk_f.reshape(bs, 1, N_HEAD, 2, DH_A), dv_f.reshape(bs, 1, N_HEAD, HEAD_W),
                      sk_f.reshape(bs, 1, N_HEAD, HEAD_W), sv_f.reshape(bs, 1, N_HEAD, HEAD_W),
                      s_new.reshape(bs, N_HEAD, DK_B, HEAD_W),
                      jnp.stack([hist_in[:, 1], g_new], axis=1)]
        for dst, val in zip(outs, prompt_new + sample_new):
            dst.append(val)

    return (xp.reshape(bp, lp, d_model), xs.reshape(bs, ls, d_model), *[jnp.stack(o) for o in outs])
```

```python
import functools
import math

import jax
import jax.numpy as jnp
from jax import lax
from jax.experimental import pallas as pl
from jax.experimental.pallas import tpu as pltpu

F32 = jnp.float32
BF16 = jnp.bfloat16
EPS = 1e-6

N_HEAD = 4
HEAD_W = 128
BR_W = N_HEAD * HEAD_W
DH_A = 64
DK_B = 64
GLA_W = N_HEAD * DK_B
GATE_RANK = 16
GLA_TAU = 16.0
N_BRANCH = 4
GLA_CHUNK = 128
GLA_SAFE_RANGE = 60.0
NEG_BIG = -1e30
EXP_UNDERFLOW = -104.0

VMEM_LIMIT_BYTES = 56 * 1024 * 1024


def _params(*sem):
    return pltpu.CompilerParams(dimension_semantics=sem, vmem_limit_bytes=VMEM_LIMIT_BYTES)


def _const_spec(shape):
    nd = len(shape)
    return pl.BlockSpec(shape, lambda *_: (0,) * nd, pipeline_mode=pl.Buffered(1))


def _full_spec(shape):
    nd = len(shape)
    return pl.BlockSpec(shape, lambda *_: (0,) * nd)


def _dot(a, b):
    return jnp.dot(a, b, preferred_element_type=F32)


def _dot_nt(a, b):
    return lax.dot_general(a, b, (((1,), (1,)), ((), ())), preferred_element_type=F32)


def _split_dot(x, m_bf16, parts):
    acc = None
    rem = x
    for _ in range(parts):
        piece = rem.astype(BF16)
        term = _dot(piece, m_bf16)
        acc = term if acc is None else acc + term
        rem = rem - piece.astype(F32)
    return acc


def _split_dot_left(m_bf16, x, parts):
    acc = None
    rem = x
    for _ in range(parts):
        piece = rem.astype(BF16)
        term = _dot(m_bf16, piece)
        acc = term if acc is None else acc + term
        rem = rem - piece.astype(F32)
    return acc


def _rms_rows(x, g):
    ms = jnp.mean(x * x, axis=-1, keepdims=True)
    return x * lax.rsqrt(ms + EPS) * g


def _log_sigmoid(x):
    return jnp.minimum(x, 0.0) - jnp.log1p(jnp.exp(-jnp.abs(x)))


def _sigmoid(x):
    return 1.0 / (1.0 + jnp.exp(-x))


def _head_rms(o, g):
    outs = []
    for h in range(N_HEAD):
        seg = o[:, h * HEAD_W:(h + 1) * HEAD_W]
        outs.append(_rms_rows(seg, g))
    return jnp.concatenate(outs, axis=-1)


_C_DQ, _C_DK, _C_DV = 0, 512, 1024
_C_GQ, _C_GK, _C_GV, _C_GG = 1536, 1792, 2048, 2560
_C_SQ, _C_SK, _C_SV = 3072, 3584, 4096
_C_MQ, _C_GA = 4608, 5120
_W_MAIN_COLS = 5248


def _proj_kernel(x_ref, g1_ref, w_ref, wa2_ref, ba_ref, qng_ref, kng_ref, mng_ref, g64_ref, g128_ref, slope_ref,
                 dq_b, dq_t, dk_f, dk1_b, dk2_b, dv_f, dv_t, gq_f, gk_f, gv_f, la_f, gg_f,
                 sq_b, sq_t, sk_f, sk_b, sv_f, sv_t, mq_b, *, attn_blk):
    xb = _rms_rows(x_ref[...], g1_ref[...]).astype(BF16)
    tm = xb.shape[0]

    def seg(off, width):
        return _dot(xb, w_ref[:, off:off + width])

    def group_norm(z, gmat_ref, gain_ref):
        ms = _split_dot(z * z, gmat_ref[...], 2)
        return z * lax.rsqrt(ms + EPS) * gain_ref[...]

    dq = group_norm(seg(_C_DQ, BR_W), g64_ref, qng_ref)
    dq_b[...] = dq.astype(BF16)
    dq_t[...] = dq.T.astype(BF16)
    dk = group_norm(seg(_C_DK, BR_W), g64_ref, kng_ref)
    dk_f[...] = dk
    lane = lax.broadcasted_iota(jnp.int32, dk.shape, 1) % HEAD_W
    pos = (lax.broadcasted_iota(jnp.int32, dk.shape, 0) + pl.program_id(0) * tm) % attn_blk
    ramp = slope_ref[...] * pos.astype(F32)
    dk1_b[...] = jnp.where(lane < DH_A, dk,
                           jnp.where(lane == DH_A, 1.0, jnp.where(lane == DH_A + 1, ramp, 0.0))).astype(BF16)
    dk2_b[...] = jnp.where(lane >= DH_A, dk,
                           jnp.where(lane == 0, 1.0, jnp.where(lane == 1, ramp, 0.0))).astype(BF16)
    dv = seg(_C_DV, BR_W)
    dv_f[...] = dv
    dv_t[...] = dv.T.astype(BF16)

    gq_f[...] = seg(_C_GQ, GLA_W) * (DK_B ** -0.5)
    gk_f[...] = seg(_C_GK, GLA_W)
    gv_f[...] = seg(_C_GV, BR_W)
    gg_f[...] = seg(_C_GG, BR_W)
    ga = seg(_C_GA, 128).astype(BF16)
    a_logit = _dot(ga, wa2_ref[...]) + ba_ref[...]
    la_f[...] = _log_sigmoid(a_logit) * (1.0 / GLA_TAU)

    sq = seg(_C_SQ, BR_W)
    sq_b[...] = sq.astype(BF16)
    sq_t[...] = sq.T.astype(BF16)
    sk = seg(_C_SK, BR_W)
    sk_f[...] = sk
    sk_b[...] = sk.astype(BF16)
    sv = seg(_C_SV, BR_W)
    sv_f[...] = sv
    sv_t[...] = sv.T.astype(BF16)

    mq_b[...] = group_norm(seg(_C_MQ, BR_W), g128_ref, mng_ref).astype(BF16)


_PROJ_OUTS = ('dq_b', 'dq_t', 'dk_f', 'dk1_b', 'dk2_b', 'dv_f', 'dv_t', 'gq', 'gk', 'gv', 'la', 'gg',
              'sq_b', 'sq_t', 'sk_f', 'sk_b', 'sv_f', 'sv_t', 'mq_b')


def _proj(x2d, lw, consts, tm, attn_blk):
    n, d = x2d.shape
    row = lambda w: pl.BlockSpec((tm, w), lambda i: (i, 0))
    widths = {'gq': GLA_W, 'gk': GLA_W, 'la': GLA_W}
    shapes, specs = [], []
    for name in _PROJ_OUTS:
        if name.endswith('_t'):
            shapes.append(jax.ShapeDtypeStruct((n // tm, BR_W, tm), BF16))
            specs.append(pl.BlockSpec((None, BR_W, tm), lambda i: (i, 0, 0)))
        else:
            w = widths.get(name, BR_W)
            shapes.append(jax.ShapeDtypeStruct((n, w), BF16 if name.endswith('_b') else F32))
            specs.append(row(w))
    ins = [x2d, lw['g1'], lw['w_main'], lw['wa2'], lw['ba'], lw['qng'], lw['kng'], lw['mng'],
           consts['g64'], consts['g128'], consts['slope_lanes']]
    outs = pl.pallas_call(
        functools.partial(_proj_kernel, attn_blk=attn_blk),
        out_shape=shapes,
        grid=(n // tm,),
        in_specs=[row(d)] + [_const_spec(a.shape) for a in ins[1:]],
        out_specs=specs,
        compiler_params=_params("parallel"),
        name="proj",
    )(*ins)
    return dict(zip(_PROJ_OUTS, outs))


def _lambda(lam_ref, lam_init):
    lp = lam_ref[...]
    s1 = jnp.sum(lp[0:1] * lp[1:2], axis=-1, keepdims=True)
    s2 = jnp.sum(lp[2:3] * lp[3:4], axis=-1, keepdims=True)
    return jnp.exp(s1) - jnp.exp(s2) + lam_init


def _diff_attn_kernel(slope_ref, win_ref, bound_ref, qt_ref, k1_ref, k2_ref, vt_ref, lam_ref, sg_ref, o_ref,
                      m1, l1, a1, m2, l2, a2, *, blk, lam_init):
    h = pl.program_id(1)
    i = pl.program_id(2)
    slope = slope_ref[h]
    qt = qt_ref[...].astype(F32) * (DH_A ** -0.5)
    sub = lax.broadcasted_iota(jnp.int32, qt.shape, 0)
    ramp = -slope * lax.broadcasted_iota(jnp.int32, qt.shape, 1).astype(F32)
    q1 = jnp.where(sub < DH_A, qt, jnp.where(sub == DH_A, ramp, jnp.where(sub == DH_A + 1, 1.0, 0.0))).astype(BF16)
    q2 = jnp.where(sub >= DH_A, qt, jnp.where(sub == 0, ramp, jnp.where(sub == 1, 1.0, 0.0))).astype(BF16)
    key = lax.broadcasted_iota(jnp.int32, (blk, blk), 0)
    qry = lax.broadcasted_iota(jnp.int32, (blk, blk), 1)

    comps = ((q1, k1_ref, m1, l1, a1), (q2, k2_ref, m2, l2, a2))
    for _, _, m_ref, l_ref, a_ref in comps:
        m_ref[...] = jnp.full(m_ref.shape, NEG_BIG, F32)
        l_ref[...] = jnp.zeros(l_ref.shape, F32)
        a_ref[...] = jnp.zeros(a_ref.shape, F32)

    def scores(j, qc, k_ref, masked):
        s = _dot(k_ref[pl.ds(pl.multiple_of(j * blk, blk), blk), :], qc)
        return jnp.where(key <= qry, s, NEG_BIG) if masked else s

    def far(j):
        return slope * ((i - j) * blk).astype(F32)

    def online():
        def block(j, masked):
            vt = vt_ref[j]
            for qc, k_ref, m_ref, l_ref, a_ref in comps:
                s = scores(j, qc, k_ref, masked)
                m_old = m_ref[...]
                m_new = jnp.maximum(m_old, jnp.max(s, axis=0, keepdims=True) - far(j))
                alpha = jnp.exp(m_old - m_new)
                p = jnp.exp(s - (m_new + far(j)))
                l_ref[...] = alpha * l_ref[...] + jnp.sum(p, axis=0, keepdims=True)
                a_ref[...] = alpha * a_ref[...] + _dot(vt, p.astype(BF16))
                m_ref[...] = m_new

        def body(j, carry):
            block(j, False)
            return carry

        lax.fori_loop(0, i, body, 0)
        block(i, True)

    def bounded():
        bound = bound_ref[0]

        def contrib(j, masked):
            vt = vt_ref[j]
            out = []
            for qc, k_ref, _, _, _ in comps:
                p = jnp.exp(scores(j, qc, k_ref, masked) - (bound + far(j)))
                out.append((jnp.sum(p, axis=0, keepdims=True), _dot(vt, p.astype(BF16))))
            return out

        def accumulate(parts):
            for c, (_, _, _, l_ref, a_ref) in enumerate(comps):
                l_ref[...] += sum(part[c][0] for part in parts)
                a_ref[...] += sum(part[c][1] for part in parts)

        first = jnp.maximum(i - (win_ref[h] - 1), 0)
        count = i - first
        odd = jnp.bitwise_and(count, 1)

        @pl.when(odd == 1)
        def _():
            accumulate([contrib(first, False)])

        def body(t, carry):
            j = first + odd + 2 * t
            accumulate([contrib(j, False), contrib(j + 1, False)])
            return carry

        lax.fori_loop(0, lax.shift_right_logical(count, 1), body, 0)
        accumulate([contrib(i, True)])

    n_keys = vt_ref.shape[0] * blk
    bound_limit = (-EXP_UNDERFLOW - math.log(n_keys) - 30.0 * math.log(2.0)) / 2.0
    lax.cond(bound_ref[0] < bound_limit, bounded, online)

    lam = _lambda(lam_ref, lam_init)
    ot = a1[...] / l1[...] - lam * (a2[...] / l2[...])
    o_ref[...] = (_rms_rows(ot.T, sg_ref[...]) * (1.0 - lam_init)).astype(BF16)


def _diff_attn(qt, k1, k2, vt, lw, consts, lam_init):
    b, n_blk, _, blk = qt.shape
    n = n_blk * blk
    qspec = pl.BlockSpec((None, None, HEAD_W, blk), lambda bi, hi, qi: (bi, qi, hi, 0))
    kspec = pl.BlockSpec((None, n, HEAD_W), lambda bi, hi, qi: (bi, 0, hi))
    vspec = pl.BlockSpec((None, n_blk, HEAD_W, blk), lambda bi, hi, qi: (bi, 0, hi, 0))
    ospec = pl.BlockSpec((None, blk, HEAD_W), lambda bi, hi, qi: (bi, qi, hi))
    smem = pl.BlockSpec(memory_space=pltpu.SMEM)
    vec = lambda rows: pltpu.VMEM((rows, blk), F32)
    return pl.pallas_call(
        functools.partial(_diff_attn_kernel, blk=blk, lam_init=lam_init),
        out_shape=jax.ShapeDtypeStruct((b, n, BR_W), BF16),
        grid=(b, N_HEAD, n_blk),
        in_specs=[smem, smem, smem, qspec, kspec, kspec, vspec,
                  _const_spec(lw['lam'].shape), _const_spec(lw['subln_g'].shape)],
        out_specs=ospec,
        scratch_shapes=[vec(1), vec(1), vec(HEAD_W), vec(1), vec(1), vec(HEAD_W)],
        compiler_params=_params("parallel", "parallel", "parallel"),
        name="diff_attn",
    )(consts['slopes'], consts['diff_window'], lw['score_bound'], qt, k1, k2, vt, lw['lam'], lw['subln_g'])


def _softplus_parts(z):
    sp = jnp.log(1.0 + jnp.exp(-jnp.abs(z)))
    return -(jnp.maximum(z, 0.0) + sp), jnp.minimum(z, 0.0) - sp


def _sb_attn_kernel(qt_ref, k_ref, vt_ref, after_ref, o_ref, r_ref, acc_ref, *, blk):
    i = pl.program_id(2)
    qt = qt_ref[...]
    key = lax.broadcasted_iota(jnp.int32, (blk, blk), 0)
    qry = lax.broadcasted_iota(jnp.int32, (blk, blk), 1)
    r_ref[...] = jnp.zeros(r_ref.shape, F32)
    acc_ref[...] = jnp.zeros(acc_ref.shape, F32)

    def block(j, masked):
        ks = pl.multiple_of(j * blk, blk)
        z = _dot(k_ref[pl.ds(ks, blk), :], qt) * (HEAD_W ** -0.5)
        log_1m, log_beta = _softplus_parts(z)
        if masked:
            valid = key < qry
            log_1m = jnp.where(valid, log_1m, 0.0)
        after = _split_dot_left(after_ref[...], log_1m, 2) + r_ref[...]
        a = jnp.exp(log_beta + after)
        if masked:
            a = jnp.where(valid, a, 0.0)
        acc_ref[...] += _dot(vt_ref[j], a.astype(BF16))
        r_ref[...] += jnp.sum(log_1m, axis=0, keepdims=True)
        return (jnp.max(r_ref[...]) > EXP_UNDERFLOW).astype(jnp.int32)

    def cond(c):
        return jnp.logical_and(c[0] >= 0, c[1] > 0)

    def body(c):
        return c[0] - 1, block(c[0], False)

    lax.while_loop(cond, body, (i - 1, block(i, True)))
    o_ref[...] = acc_ref[...].T.astype(BF16)


def _sb_attn(qt, k, vt, consts):
    b, n_blk, _, blk = qt.shape
    n = n_blk * blk
    qspec = pl.BlockSpec((None, None, HEAD_W, blk), lambda bi, hi, qi: (bi, qi, hi, 0))
    kspec = pl.BlockSpec((None, n, HEAD_W), lambda bi, hi, qi: (bi, 0, hi))
    vspec = pl.BlockSpec((None, n_blk, HEAD_W, blk), lambda bi, hi, qi: (bi, 0, hi, 0))
    ospec = pl.BlockSpec((None, blk, HEAD_W), lambda bi, hi, qi: (bi, qi, hi))
    after = consts['after'][blk]
    return pl.pallas_call(
        functools.partial(_sb_attn_kernel, blk=blk),
        out_shape=jax.ShapeDtypeStruct((b, n, BR_W), BF16),
        grid=(b, N_HEAD, n_blk),
        in_specs=[qspec, kspec, vspec, _const_spec(after.shape)],
        out_specs=ospec,
        scratch_shapes=[pltpu.VMEM((1, blk), F32), pltpu.VMEM((HEAD_W, blk), F32)],
        compiler_params=_params("parallel", "parallel", "parallel"),
        name="sb_attn",
    )(qt, k, vt, after)


def _lane_head(shape, width):
    return lax.broadcasted_iota(jnp.int32, shape, 1) // width


def _column(xt, t):
    lane = lax.broadcasted_iota(jnp.int32, xt.shape, 1)
    return jnp.sum(jnp.where(lane == t, xt, 0.0), axis=1, keepdims=True)


def _gla_token(state, decay_col, k_col, q_col, v_row):
    v_rows = jnp.concatenate(
        [jnp.broadcast_to(v_row[:, h * HEAD_W:(h + 1) * HEAD_W], (DK_B, HEAD_W)) for h in range(N_HEAD)], axis=0)
    state = decay_col * state + k_col * v_rows
    qs = q_col * state
    o = jnp.concatenate(
        [jnp.sum(qs[h * DK_B:(h + 1) * DK_B], axis=0, keepdims=True) for h in range(N_HEAD)], axis=-1)
    return state, o


def _gla_finish(o, gg, ng):
    return (_head_rms(o, ng) * (gg * _sigmoid(gg))).astype(BF16)


def _gla_chunk_kernel(q_ref, k_ref, v_ref, la_ref, gg_ref, ng_ref, ltri_ref, o_ref, s_out_ref, s_scr, o_scr):
    c = GLA_CHUNK

    @pl.when(pl.program_id(1) == 0)
    def _():
        s_scr[...] = jnp.zeros(s_scr.shape, F32)

    q = q_ref[...]
    k = k_ref[...]
    la = la_ref[...]
    cb = _split_dot_left(ltri_ref[...], la, 3)
    cbm = cb - cb[c // 2 - 1:c // 2, :]
    spread = jnp.max(jnp.abs(cbm))

    def factored():
        vb = v_ref[...].astype(BF16)
        state = s_scr[...]
        state_b = state.astype(BF16)
        qt = q * jnp.exp(cbm)
        kt = (k * jnp.exp(-cbm)).astype(BF16)
        qs = q * jnp.exp(cb)
        cb_t = cb.T
        last = cb_t[:, c - 1:c]
        kh_t = (k.T * jnp.exp(last - cb_t)).astype(BF16)
        head = _lane_head((c, GLA_W), DK_B)
        causal = (lax.broadcasted_iota(jnp.int32, (c, c), 1) <= lax.broadcasted_iota(jnp.int32, (c, c), 0))
        outs, upd = [], []
        for h in range(N_HEAD):
            sel = head == h
            vh = vb[:, h * HEAD_W:(h + 1) * HEAD_W]
            attn = jnp.where(causal, _dot_nt(jnp.where(sel, qt, 0.0).astype(BF16), kt), 0.0)
            outs.append(_dot(attn.astype(BF16), vh) + _dot(jnp.where(sel, qs, 0.0).astype(BF16), state_b))
            upd.append(_dot(kh_t[h * DK_B:(h + 1) * DK_B, :], vh))
        o_scr[...] = jnp.concatenate(outs, axis=-1)
        s_scr[...] = jnp.exp(last) * state + jnp.concatenate(upd, axis=0)

    def recurrent():
        q_t = q.T
        k_t = k.T
        decay_t = jnp.exp(la).T

        def step(t, state):
            state, o = _gla_token(state, _column(decay_t, t), _column(k_t, t), _column(q_t, t),
                                  v_ref[pl.ds(t, 1), :])
            o_scr[pl.ds(t, 1), :] = o
            return state

        s_scr[...] = lax.fori_loop(0, c, step, s_scr[...])

    lax.cond(spread <= GLA_SAFE_RANGE, factored, recurrent)

    o_ref[...] = _gla_finish(o_scr[...], gg_ref[...], ng_ref[...])
    s_out_ref[...] = s_scr[...]


def _gla_chunked(gq, gk, gv, la, gg, lw, consts):
    b, n, _ = gq.shape
    c = GLA_CHUNK
    blk = lambda w: pl.BlockSpec((None, c, w), lambda bi, ci: (bi, ci, 0))
    ltri = consts['ltri']
    return pl.pallas_call(
        _gla_chunk_kernel,
        out_shape=[jax.ShapeDtypeStruct((b, n, BR_W), BF16),
                   jax.ShapeDtypeStruct((b, GLA_W, HEAD_W), F32)],
        grid=(b, n // c),
        in_specs=[blk(GLA_W), blk(GLA_W), blk(BR_W), blk(GLA_W), blk(BR_W),
                  _const_spec(lw['gla_ng'].shape), _const_spec(ltri.shape)],
        out_specs=[blk(BR_W), pl.BlockSpec((None, GLA_W, HEAD_W), lambda bi, ci: (bi, 0, 0))],
        scratch_shapes=[pltpu.VMEM((GLA_W, HEAD_W), F32), pltpu.VMEM((c, BR_W), F32)],
        compiler_params=_params("parallel", "arbitrary"),
        name="gla_chunked",
    )(gq, gk, gv, la, gg, lw['gla_ng'], ltri)


def _gla_step_kernel(q_ref, k_ref, la_ref, v_ref, gg_ref, ng_ref, s_ref, o_ref, s_out_ref, qt_scr, kt_scr, dt_scr):
    b = pl.program_id(0)

    @pl.when(b == 0)
    def _():
        qt_scr[...] = q_ref[...].T
        kt_scr[...] = k_ref[...].T
        dt_scr[...] = jnp.exp(la_ref[...]).T

    state, o = _gla_token(s_ref[...], _column(dt_scr[...], b), _column(kt_scr[...], b), _column(qt_scr[...], b),
                          v_ref[pl.ds(b, 1), :])
    s_out_ref[...] = state
    o_ref[...] = _gla_finish(o, gg_ref[pl.ds(b, 1), :], ng_ref[...])


def _gla_step(gq, gk, la, gv, gg, state, layer, lw):
    nb = gq.shape[0]
    sspec = pl.BlockSpec((None, GLA_W, HEAD_W), lambda bi: (bi, 0, 0))
    sin = pl.BlockSpec((None, None, GLA_W, HEAD_W), lambda bi: (layer, bi, 0, 0))
    t_scr = pltpu.VMEM((GLA_W, nb), F32)
    return pl.pallas_call(
        _gla_step_kernel,
        out_shape=[jax.ShapeDtypeStruct((nb, 1, BR_W), BF16),
                   jax.ShapeDtypeStruct((nb, GLA_W, HEAD_W), F32)],
        grid=(nb,),
        in_specs=[_const_spec(gq.shape), _const_spec(gk.shape), _const_spec(la.shape), _const_spec(gv.shape),
                  _const_spec(gg.shape), _const_spec(lw['gla_ng'].shape), sin],
        out_specs=[pl.BlockSpec((None, 1, BR_W), lambda bi: (bi, 0, 0)), sspec],
        scratch_shapes=[t_scr, t_scr, t_scr],
        compiler_params=_params("arbitrary"),
        name="gla_step",
    )(gq, gk, la, gv, gg, lw['gla_ng'], state)


def _mem_kv_kernel(m_ref, g_ref, w_ref, kng_ref, g128_ref, k_f, v_f):
    xb = _rms_rows(m_ref[...], g_ref[...]).astype(BF16)
    k = _dot(xb, w_ref[:, 0:BR_W])
    ms = _split_dot(k * k, g128_ref[...], 2)
    k_f[...] = k * lax.rsqrt(ms + EPS) * kng_ref[...]
    v_f[...] = _dot(xb, w_ref[:, BR_W:2 * BR_W])


def _mem_kv(mem2d, lw, consts):
    n = mem2d.shape[0]
    ins = [mem2d, lw['mem_g'], lw['w_mem'], lw['kmg'], consts['g128']]
    return pl.pallas_call(
        _mem_kv_kernel,
        out_shape=[jax.ShapeDtypeStruct((n, BR_W), F32)] * 2,
        grid=(1,),
        in_specs=[_const_spec(a.shape) for a in ins],
        out_specs=[_full_spec((n, BR_W))] * 2,
        compiler_params=_params("arbitrary"),
        name="mem_kv",
    )(*ins)


def _mem_attn_kernel(q_ref, k_ref, v_ref, o_ref, *, rows, interleaved):
    q = q_ref[...]
    if rows < 8:
        q = jnp.broadcast_to(q[0:1], (8, q.shape[1]))
    outs = []
    for h in range(N_HEAD):
        sl = slice(h * HEAD_W, (h + 1) * HEAD_W)
        if interleaved:
            n_mem = k_ref.shape[0] // N_HEAD
            kh = k_ref[pl.ds(h, n_mem, stride=N_HEAD), :].astype(BF16)
            vh = v_ref[pl.ds(h, n_mem, stride=N_HEAD), :].astype(BF16)
        else:
            kh = k_ref[:, sl].astype(BF16)
            vh = v_ref[:, sl].astype(BF16)
        s = _dot_nt(q[:, sl], kh) * (HEAD_W ** -0.5)
        p = jnp.exp(s - jnp.max(s, axis=-1, keepdims=True))
        o = _dot(p.astype(BF16), vh)
        outs.append(o / jnp.sum(p, axis=-1, keepdims=True))
    o_ref[...] = jnp.concatenate(outs, axis=-1)[0:rows].astype(BF16)


def _mem_attn(q, k, v, layer, tm):
    b, n, _ = q.shape
    interleaved = k.shape[3] == HEAD_W
    qspec = pl.BlockSpec((None, tm, BR_W), lambda bi, ri: (bi, ri, 0))
    kvspec = pl.BlockSpec((None, None) + k.shape[2:], lambda bi, ri: (layer, bi, 0, 0))
    return pl.pallas_call(
        functools.partial(_mem_attn_kernel, rows=tm, interleaved=interleaved),
        out_shape=jax.ShapeDtypeStruct((b, n, BR_W), BF16),
        grid=(b, n // tm),
        in_specs=[qspec, kvspec, kvspec],
        out_specs=qspec,
        compiler_params=_params("parallel", "parallel"),
        name="mem_attn",
    )(q, k, v)


def _merge_kernel(x_ref, g1_ref, wg_ref, bg_ref, b0_ref, b1_ref, b2_ref, b3_ref, wbr_ref, wout_ref, o_ref):
    x = x_ref[...]
    d = x.shape[1]
    xb = _rms_rows(x, g1_ref[...]).astype(BF16)
    mixed = None
    for i, br_ref in enumerate((b0_ref, b1_ref, b2_ref, b3_ref)):
        gate = _sigmoid(_dot(xb, wg_ref[:, i * d:(i + 1) * d]) + bg_ref[:, i * d:(i + 1) * d])
        term = gate * _dot(br_ref[...], wbr_ref[i])
        mixed = term if mixed is None else mixed + term
    o_ref[...] = x + _dot(mixed.astype(BF16), wout_ref[...])


def _merge(x2d, branches, lw, tm):
    n, d = x2d.shape
    row = lambda w: pl.BlockSpec((tm, w), lambda i: (i, 0))
    ws = [lw['g1'], lw['w_gate'], lw['b_gate']]
    ws2 = [lw['w_br'], lw['w_out']]
    return pl.pallas_call(
        _merge_kernel,
        out_shape=jax.ShapeDtypeStruct((n, d), F32),
        grid=(n // tm,),
        in_specs=[row(d)] + [_const_spec(a.shape) for a in ws] + [row(BR_W)] * N_BRANCH
                 + [_const_spec(a.shape) for a in ws2],
        out_specs=row(d),
        compiler_params=_params("parallel"),
        name="merge",
    )(x2d, *ws, *branches, *ws2)


FF_CHUNK = 256
CARRY_ROWS = 8


def _gelu_tanh(x):
    return 0.5 * x * (1.0 + jnp.tanh(math.sqrt(2.0 / math.pi) * (x + 0.044715 * (x * x * x))))


def _ffn_kernel(*refs, seq_mode, d_ff):
    if seq_mode:
        x_ref, g2_ref, wu_ref, wg_ref, cw_ref, cb_ref, wd_ref, o_ref, hist_ref, carry = refs
    else:
        x_ref, g2_ref, wu_ref, wg_ref, cw_ref, cb_ref, wd_ref, h0_ref, h1_ref, o_ref, g_out_ref = refs
    x = x_ref[...]
    tm = x.shape[0]
    hb = _rms_rows(x, g2_ref[...]).astype(BF16)

    if seq_mode:
        @pl.when(pl.program_id(1) == 0)
        def _():
            carry[...] = jnp.zeros(carry.shape, F32)
        rowid = lax.broadcasted_iota(jnp.int32, (tm, FF_CHUNK), 0)

    y = None
    for c0 in range(0, d_ff, FF_CHUNK):
        cs = slice(c0, c0 + FF_CHUNK)
        u = _dot(hb, wu_ref[:, cs])
        g = _dot(hb, wg_ref[:, cs])
        if seq_mode:
            prev = carry[:, cs]
            p1 = prev[CARRY_ROWS - 1:CARRY_ROWS]
            p2 = prev[CARRY_ROWS - 2:CARRY_ROWS - 1]
            g1 = jnp.where(rowid == 0, p1, pltpu.roll(g, 1, 0))
            g2 = jnp.where(rowid == 0, p2, jnp.where(rowid == 1, p1, pltpu.roll(g, 2, 0)))
            carry[:, cs] = g[tm - CARRY_ROWS:tm]
        else:
            g1 = h1_ref[:, cs]
            g2 = h0_ref[:, cs]
            g_out_ref[:, cs] = g
        gc = g2 * cw_ref[0:1, cs] + g1 * cw_ref[1:2, cs] + g * cw_ref[2:3, cs] + cb_ref[:, cs]
        term = _dot((_gelu_tanh(gc) * u).astype(BF16), wd_ref[cs, :])
        y = term if y is None else y + term
    o_ref[...] = x + y
    if seq_mode:
        hist_ref[...] = carry[CARRY_ROWS - 2:CARRY_ROWS, :]


def _ffn_seq(x3d, lw, tm):
    b, n, d = x3d.shape
    d_ff = lw['w_u'].shape[1]
    ws = [lw['g2'], lw['w_u'], lw['w_g'], lw['conv_w'], lw['conv_b'], lw['w_down']]
    xspec = pl.BlockSpec((None, tm, d), lambda bi, ri: (bi, ri, 0))
    return pl.pallas_call(
        functools.partial(_ffn_kernel, seq_mode=True, d_ff=d_ff),
        out_shape=[jax.ShapeDtypeStruct((b, n, d), F32), jax.ShapeDtypeStruct((b, 2, d_ff), F32)],
        grid=(b, n // tm),
        in_specs=[xspec] + [_const_spec(a.shape) for a in ws],
        out_specs=[xspec, pl.BlockSpec((None, 2, d_ff), lambda bi, ri: (bi, 0, 0))],
        scratch_shapes=[pltpu.VMEM((CARRY_ROWS, d_ff), F32)],
        compiler_params=_params("parallel", "arbitrary"),
        name="ffn_seq",
    )(x3d, *ws)


def _ffn_step(x2d, hist0, hist1, lw):
    n, d = x2d.shape
    d_ff = lw['w_u'].shape[1]
    ws = [lw['g2'], lw['w_u'], lw['w_g'], lw['conv_w'], lw['conv_b'], lw['w_down']]
    ins = [x2d, *ws, hist0, hist1]
    return pl.pallas_call(
        functools.partial(_ffn_kernel, seq_mode=False, d_ff=d_ff),
        out_shape=[jax.ShapeDtypeStruct((n, d), F32), jax.ShapeDtypeStruct((n, d_ff), F32)],
        grid=(1,),
        in_specs=[_const_spec(a.shape) for a in ins],
        out_specs=[_full_spec((n, d)), _full_spec((n, d_ff))],
        compiler_params=_params("arbitrary"),
        name="ffn_step",
    )(*ins)


DEC_ROWS = 8


def _decode_attn_kernel(pt_ref, *refs, pages_per_step, n_pages, page, lam_init):
    p = pages_per_step
    (slope_ref, dq_ref, dk_ref, dv_ref, sq_ref, lam_ref, sg_ref, u_ref) = refs[:8]
    kd = refs[8:8 + p]
    vd = refs[8 + p:8 + 2 * p]
    ks = refs[8 + 2 * p:8 + 3 * p]
    vs = refs[8 + 3 * p:8 + 4 * p]
    od_ref, os_ref, m_ref, l_ref, acc_ref, r_ref, accs_ref = refs[8 + 4 * p:]
    j = pl.program_id(1)
    shape = (DEC_ROWS, BR_W)
    row = lax.broadcasted_iota(jnp.int32, shape, 0)
    lane = lax.broadcasted_iota(jnp.int32, shape, 1)
    dq = jnp.broadcast_to(dq_ref[...].astype(F32), shape)
    qd_f = jnp.where(lane // DH_A == row, dq, 0.0)
    qd = qd_f.astype(BF16)
    sq = jnp.broadcast_to(sq_ref[...].astype(F32), shape)
    qs = jnp.where(lane // HEAD_W == row, sq, 0.0).astype(BF16)
    slopes = slope_ref[...][:, 0:1]
    past = n_pages * page

    @pl.when(j == 0)
    def _():
        s_self = jnp.sum(qd_f * dk_ref[...], axis=-1, keepdims=True) * (DH_A ** -0.5)
        m_ref[...] = s_self
        l_ref[...] = jnp.ones(l_ref.shape, F32)
        acc_ref[...] = jnp.broadcast_to(dv_ref[...], shape)
        r_ref[...] = jnp.zeros(r_ref.shape, F32)
        accs_ref[...] = jnp.zeros(accs_ref.shape, F32)

    col = lax.broadcasted_iota(jnp.int32, (DEC_ROWS, page), 1)
    prow = lax.broadcasted_iota(jnp.int32, (DEC_ROWS, page), 0)

    def head_rows(ref, h):
        return ref[pl.ds(h, page, stride=N_HEAD), :].astype(BF16)

    def per_head(w, v_ref):
        return jnp.concatenate([_dot(w, head_rows(v_ref, h)) for h in range(N_HEAD)], axis=-1)

    span = p * page
    pages = [slice(t * page, (t + 1) * page) for t in range(p)]

    dist = (past - j * span - lax.broadcasted_iota(jnp.int32, (DEC_ROWS, span), 1)).astype(F32)
    s = jnp.concatenate([_dot(qd, kd[t][...].astype(BF16)) for t in range(p)], axis=-1)
    s = s * (DH_A ** -0.5) - slopes * dist
    m_old = m_ref[...]
    m_new = jnp.maximum(m_old, jnp.max(s, axis=-1, keepdims=True))
    alpha = jnp.exp(m_old - m_new)
    pr = jnp.exp(s - m_new)
    l_ref[...] = alpha * l_ref[...] + jnp.sum(pr, axis=-1, keepdims=True)
    pr = pr.astype(BF16)
    acc_ref[...] = alpha * acc_ref[...] + sum(per_head(pr[:, pages[t]], vd[t]) for t in range(p))
    m_ref[...] = m_new

    zs = []
    for t in range(p):
        z = None
        for h in range(N_HEAD):
            zh = _dot_nt(qs[:, h * HEAD_W:(h + 1) * HEAD_W], head_rows(ks[t], h))
            z = zh if z is None else jnp.where(prow == h, zh, z)
        zs.append(z)
    log_1m, log_beta = _softplus_parts(jnp.concatenate(zs, axis=-1) * (HEAD_W ** -0.5))
    r = r_ref[...]
    upd = None
    for t in range(p):
        after = _split_dot(log_1m[:, pages[t]], u_ref[...], 2) + r
        a = jnp.exp(log_beta[:, pages[t]] + after)
        term = per_head(a.astype(BF16), vs[t])
        upd = term if upd is None else upd + term
        r = r + jnp.sum(log_1m[:, pages[t]], axis=-1, keepdims=True)
    accs_ref[...] += upd
    r_ref[...] = r

    @pl.when(j == pl.num_programs(1) - 1)
    def _():
        lam = _lambda(lam_ref, lam_init)
        coef = jnp.where(row % 2 == 0, 1.0, -lam)
        coef = jnp.where(lane // HEAD_W == row // 2, coef, 0.0)
        o = jnp.sum(acc_ref[...] / l_ref[...] * coef, axis=0, keepdims=True)
        od_ref[...] = (_head_rms(o, sg_ref[...]) * (1.0 - lam_init)).astype(BF16)
        keep = jnp.where(lane // HEAD_W == row, accs_ref[...], 0.0)
        os_ref[...] = jnp.sum(keep, axis=0, keepdims=True).astype(BF16)


def _decode_attn(page_table, dq, dk, dv, sq, caches, layer, lw, consts, lam_init, pages_per_step):
    nb, n_pages = page_table.shape
    cdk, cdv, csk, csv = caches
    page = cdk.shape[3]
    p = pages_per_step
    steps = n_pages // p
    rowspec = pl.BlockSpec((None, 1, BR_W), lambda b, j, pt: (b, 0, 0))
    cspec = lambda a: pl.BlockSpec(a.shape, lambda b, j, pt: (0,) * a.ndim)

    def page_spec(t, reverse, rows, cols):
        def index(b, j, pt):
            logical = j * p + t
            if reverse:
                logical = n_pages - 1 - logical
            return (layer, pt[b, logical], 0, 0)
        return pl.BlockSpec((None, None, rows, cols), index)

    u = consts['suffix'][page]
    small = [consts['slope_rows'], dq, dk, dv, sq, lw['lam'], lw['subln_g'], u]
    in_specs = ([cspec(small[0])] + [rowspec] * 4 + [cspec(a) for a in small[5:]]
                + [page_spec(t, False, BR_W, page) for t in range(p)]
                + [page_spec(t, False, N_HEAD * page, HEAD_W) for t in range(p)]
                + [page_spec(t, True, N_HEAD * page, HEAD_W) for t in range(p)] * 2)
    vec = lambda w: pltpu.VMEM((DEC_ROWS, w), F32)
    grid_spec = pltpu.PrefetchScalarGridSpec(
        num_scalar_prefetch=1, grid=(nb, steps), in_specs=in_specs,
        out_specs=[rowspec, rowspec],
        scratch_shapes=[vec(1), vec(1), vec(BR_W), vec(1), vec(BR_W)])
    return pl.pallas_call(
        functools.partial(_decode_attn_kernel, pages_per_step=p, n_pages=n_pages, page=page, lam_init=lam_init),
        out_shape=[jax.ShapeDtypeStruct((nb, 1, BR_W), BF16)] * 2,
        grid_spec=grid_spec,
        compiler_params=_params("parallel", "arbitrary"),
        name="decode_attn",
    )(page_table, *small, *([cdk] * p), *([cdv] * p), *([csk] * p), *([csv] * p))


def _block_diag_mean(width, group):
    idx = jnp.arange(width) // group
    return jnp.where(idx[:, None] == idx[None, :], 1.0 / group, 0.0).astype(BF16)


def _suffix_matrix(n):
    idx = jnp.arange(n)
    return (idx[:, None] > idx[None, :]).astype(BF16)


def _row_tile(n, cap):
    t = min(n, cap)
    while n % t:
        t //= 2
    return t


def kernel(x_prompt, x_sample, mem_prompt, cache_diff_k, cache_diff_v, cache_sb_k, cache_sb_v, cache_mem_k, cache_mem_v, state_gla, state_ffn_conv, page_table, norm1_g, w_in, b_gate, qn_a_g, kn_a_g, lam_q1, lam_k1, lam_q2, lam_k2, subln_a_g, w_alpha2, b_alpha, gla_norm_g, mem_norm_g, w_mem_kv, qn_m_g, kn_m_g, w_br, w_out, norm2_g, w_up, conv_w, conv_b, w_down):
    depth, d_model, _ = w_in.shape
    bp, lp, _ = x_prompt.shape
    bs, ls, _ = x_sample.shape
    assert ls == 1, "the sample group decodes one token per sequence"
    n_mem = mem_prompt.shape[1]
    n_phys, page = cache_diff_k.shape[1], cache_diff_k.shape[2]
    n_pages = page_table.shape[1]
    d_ff = w_down.shape[1]

    attn_blk = _row_tile(lp, 256)
    pages_per_step = _row_tile(n_pages, 8)
    slopes = [2.0 ** (-8.0 * (h + 1) / N_HEAD) for h in range(N_HEAD)]
    assert all(math.frexp(s)[0] == 0.5 for s in slopes) and attn_blk <= 256
    consts = {
        'g64': _block_diag_mean(BR_W, DH_A),
        'g128': _block_diag_mean(BR_W, HEAD_W),
        'ltri': (jnp.arange(GLA_CHUNK)[:, None] >= jnp.arange(GLA_CHUNK)[None, :]).astype(BF16),
        'suffix': {page: _suffix_matrix(page)},
        'after': {attn_blk: _suffix_matrix(attn_blk).T},
        'slopes': jnp.asarray(slopes, F32),
    }
    consts['diff_window'] = jnp.asarray(
        [min(lp // attn_blk, int((-EXP_UNDERFLOW / s + attn_blk - 1) // attn_blk) + 1) for s in slopes], jnp.int32)
    consts['slope_rows'] = jnp.broadcast_to(jnp.repeat(consts['slopes'], 2)[:, None], (DEC_ROWS, 128))
    consts['slope_lanes'] = jnp.repeat(consts['slopes'], HEAD_W).reshape(1, BR_W)

    caches = [jnp.transpose(cache_diff_k, (0, 1, 3, 4, 5, 2)).reshape(depth, n_phys, BR_W, page)]
    caches += [c.reshape(depth, n_phys, page * N_HEAD, HEAD_W) for c in (cache_diff_v, cache_sb_k, cache_sb_v)]
    mem_k = cache_mem_k.reshape(depth, bs, n_mem * N_HEAD, HEAD_W)
    mem_v = cache_mem_v.reshape(depth, bs, n_mem * N_HEAD, HEAD_W)
    gla_state = state_gla.reshape(depth, bs, GLA_W, HEAD_W)

    splits = [0]
    for w in (512, 512, 512, 256, 256, 512, GATE_RANK, 512, 512, 512, 512, 512, N_BRANCH * d_model):
        splits.append(splits[-1] + w)
    (c_dq, c_dk, c_dv, c_gq, c_gk, c_gv, c_ga, c_gg, c_sq, c_sk, c_sv, c_mq, c_gl, c_end) = splits

    def tile_gain(g, reps):
        return jnp.tile(g, reps).reshape(1, -1)

    xp = x_prompt.reshape(bp * lp, d_model)
    xs = x_sample.reshape(bs * ls, d_model)
    mem2d = mem_prompt.reshape(bp * n_mem, d_model)
    tm_p = _row_tile(bp * lp, 256)
    tm_s = _row_tile(bs, 256)
    tm_f = _row_tile(lp, 256)

    outs = [[] for _ in range(14)]
    for l in range(depth):
        lam_init = 0.8 - 0.6 * math.exp(-0.3 * l)
        wl = w_in[l]
        ga_pad = jnp.zeros((d_model, 128 - GATE_RANK), wl.dtype)
        lw = {
            'g1': norm1_g[l].reshape(1, -1),
            'w_main': jnp.concatenate(
                [wl[:, c_dq:c_ga], wl[:, c_gg:c_gl], wl[:, c_ga:c_gg], ga_pad], axis=1).astype(BF16),
            'wa2': jnp.concatenate([w_alpha2[l], jnp.zeros((128 - GATE_RANK, GLA_W), F32)], axis=0).astype(BF16),
            'ba': b_alpha[l].reshape(1, -1),
            'qng': tile_gain(qn_a_g[l], 2 * N_HEAD),
            'kng': tile_gain(kn_a_g[l], 2 * N_HEAD),
            'mng': tile_gain(qn_m_g[l], N_HEAD),
            'kmg': tile_gain(kn_m_g[l], N_HEAD),
            'lam': jnp.stack([lam_q1[l], lam_k1[l], lam_q2[l], lam_k2[l]]),
            'score_bound': (1.02 * math.sqrt(DH_A) * jnp.max(jnp.abs(qn_a_g[l])) * jnp.max(jnp.abs(kn_a_g[l]))
                            ).reshape(1).astype(F32),
            'subln_g': subln_a_g[l].reshape(1, -1),
            'gla_ng': gla_norm_g[l].reshape(1, -1),
            'mem_g': mem_norm_g[l].reshape(1, -1),
            'w_mem': w_mem_kv[l].astype(BF16),
            'w_gate': wl[:, c_gl:c_end].astype(BF16),
            'b_gate': b_gate[l].reshape(1, -1),
            'w_br': w_br[l].astype(BF16),
            'w_out': w_out[l].astype(BF16),
            'g2': norm2_g[l].reshape(1, -1),
            'w_u': w_up[l][:, :d_ff].astype(BF16),
            'w_g': w_up[l][:, d_ff:].astype(BF16),
            'conv_w': conv_w[l],
            'conv_b': conv_b[l].reshape(1, -1),
            'w_down': w_down[l].astype(BF16),
        }
        assert lw['w_main'].shape[1] == _W_MAIN_COLS

        pr = _proj(xp, lw, consts, attn_blk, attn_blk)
        r3 = lambda a: a.reshape(bp, lp, a.shape[-1])
        blocks = lambda a: a.reshape(bp, lp // attn_blk, BR_W, attn_blk)
        mk_f, mv_f = _mem_kv(mem2d, lw, consts)
        o_a = _diff_attn(blocks(pr['dq_t']), r3(pr['dk1_b']), r3(pr['dk2_b']), blocks(pr['dv_t']), lw, consts, lam_init)
        o_b, s_new = _gla_chunked(r3(pr['gq']), r3(pr['gk']), r3(pr['gv']), r3(pr['la']), r3(pr['gg']), lw, consts)
        o_c = _sb_attn(blocks(pr['sq_t']), r3(pr['sk_b']), blocks(pr['sv_t']), consts)
        o_m = _mem_attn(r3(pr['mq_b']), mk_f.reshape(1, bp, n_mem, BR_W), mv_f.reshape(1, bp, n_mem, BR_W), 0, tm_f)
        flat = lambda a: a.reshape(bp * lp, BR_W)
        x1 = _merge(xp, [flat(o_a), flat(o_b), flat(o_c), flat(o_m)], lw, tm_p)
        x2, hist = _ffn_seq(x1.reshape(bp, lp, d_model), lw, tm_f)
        xp = x2.reshape(bp * lp, d_model)
        prompt_new = [pr['dk_f'].reshape(bp, lp, N_HEAD, 2, DH_A), pr['dv_f'].reshape(bp, lp, N_HEAD, HEAD_W),
                      pr['sk_f'].reshape(bp, lp, N_HEAD, HEAD_W), pr['sv_f'].reshape(bp, lp, N_HEAD, HEAD_W),
                      mk_f.reshape(bp, n_mem, N_HEAD, HEAD_W), mv_f.reshape(bp, n_mem, N_HEAD, HEAD_W),
                      s_new.reshape(bp, N_HEAD, DK_B, HEAD_W), hist]

        ps = _proj(xs, lw, consts, tm_s, attn_blk)
        r1 = lambda a: a.reshape(bs, 1, a.shape[-1])
        o_a, o_c = _decode_attn(page_table, r1(ps['dq_b']), r1(ps['dk_f']), r1(ps['dv_f']), r1(ps['sq_b']), caches,
                                l, lw, consts, lam_init, pages_per_step)
        o_b, s_new = _gla_step(ps['gq'], ps['gk'], ps['la'], ps['gv'], ps['gg'], gla_state, l, lw)
        o_m = _mem_attn(r1(ps['mq_b']), mem_k, mem_v, l, 1)
        flat = lambda a: a.reshape(bs, BR_W)
        x1 = _merge(xs, [flat(o_a), flat(o_b), flat(o_c), flat(o_m)], lw, tm_s)
        hist_in = state_ffn_conv[l]
        xs, g_new = _ffn_step(x1, hist_in[:, 0], hist_in[:, 1], lw)
        sample_new = [ps['dk_f'].reshape(bs, 1, N_HEAD, 2, DH_A), ps['dv_f'].reshape(bs, 1, N_HEAD, HEAD_W),
                      ps['sk_f'].reshape(bs, 1, N_HEAD, HEAD_W), ps['sv_f'].reshape(bs, 1, N_HEAD, HEAD_W),
                      s_new.reshape(bs, N_HEAD, DK_B, HEAD_W),
                      jnp.stack([hist_in[:, 1], g_new], axis=1)]
        for dst, val in zip(outs, prompt_new + sample_new):
            dst.append(val)

    return (xp.reshape(bp, lp, d_model), xs.reshape(bs, ls, d_model), *[jnp.stack(o) for o in outs])
```
